```python
import math
import jax, jax.numpy as jnp
from jax import lax
import numpy as np

D_MODEL = 1024
BATCH = 8
SEQ = 2048
DEPTH = 4

D_FF = 2816
N_NORMS = 6
EPS = 1e-6
HEAD_DIM = 64
CONV_CH = 512
CONV_K = 31
SWA_HQ = 8
SWA_HKV = 2
SWA_G = SWA_HQ // SWA_HKV
WINDOW = 128
BLK = 128
FOX_H = 16
AB_IN = 2 * CONV_CH + SWA_HQ * HEAD_DIM + 2 * SWA_HKV * HEAD_DIM
AB_OUT = CONV_CH + SWA_HQ * HEAD_DIM
FOX_IN = 3 * FOX_H * HEAD_DIM + FOX_H
FOX_OUT = FOX_H * HEAD_DIM
NEG = -1e30

kernel_name = "hybrid_conv_swa_fox_macaron"


def _rmsnorm(x, g):
    xf = x.astype(jnp.float32)
    y = xf * lax.rsqrt(jnp.mean(xf * xf, axis=-1, keepdims=True) + EPS)
    return (y * g.astype(jnp.float32)).astype(x.dtype)


def _swiglu(x, wg, wu, wd):
    return (jax.nn.silu(x @ wg) * (x @ wu)) @ wd


def _alibi_slopes(n):
    return jnp.asarray([2.0 ** (-8.0 * (i + 1) / n) for i in range(n)], dtype=jnp.float32)


def _conv_module(u, gate, w, b, ln_g, ln_b):
    a = u * jax.nn.sigmoid(gate)
    y = lax.conv_general_dilated(
        a, w.reshape(CONV_K, 1, CONV_CH).astype(a.dtype),
        window_strides=(1,), padding=[(CONV_K - 1, 0)],
        dimension_numbers=("NWC", "WIO", "NWC"),
        feature_group_count=CONV_CH) + b
    yf = y.astype(jnp.float32)
    mu = jnp.mean(yf, axis=-1, keepdims=True)
    var = jnp.mean(jnp.square(yf - mu), axis=-1, keepdims=True)
    yf = (yf - mu) * lax.rsqrt(var + EPS) * ln_g.astype(jnp.float32) + ln_b.astype(jnp.float32)
    return jax.nn.silu(yf).astype(u.dtype)


def _swa_sinks(q, k, v, sinks):
    b, s, _, dh = q.shape
    nb = s // BLK
    qb = q.reshape(b, nb, BLK, SWA_HKV, SWA_G, dh)
    kb = k.reshape(b, nb, BLK, SWA_HKV, dh)
    vb = v.reshape(b, nb, BLK, SWA_HKV, dh)
    pad = ((0, 0), (1, 0), (0, 0), (0, 0), (0, 0))
    kk = jnp.concatenate([jnp.pad(kb, pad)[:, :-1], kb], axis=2)
    vv = jnp.concatenate([jnp.pad(vb, pad)[:, :-1], vb], axis=2)
    sc = jnp.einsum("bnqhgd,bnkhd->bnhgqk", qb, kk).astype(jnp.float32) / math.sqrt(dh)
    t_loc = jnp.arange(BLK)[:, None]
    s_loc = jnp.arange(2 * BLK)[None, :]
    dist = t_loc + BLK - s_loc
    valid = (dist >= 0) & (dist < WINDOW)
    mask = valid[None] & ((jnp.arange(nb)[:, None, None] > 0) | (s_loc >= BLK)[None])
    slopes = _alibi_slopes(SWA_HQ).reshape(SWA_HKV, SWA_G)
    sc = sc - slopes[:, :, None, None] * dist.astype(jnp.float32)
    sc = jnp.where(mask[None, :, None, None], sc, NEG)
    sink = jnp.broadcast_to(sinks.astype(jnp.float32).reshape(SWA_HKV, SWA_G)[None, None, :, :, None, None],
                            sc.shape[:-1] + (1,))
    p = jax.nn.softmax(jnp.concatenate([sc, sink], axis=-1), axis=-1)[..., :-1]
    o = jnp.einsum("bnhgqk,bnkhd->bnqhgd", p.astype(v.dtype), vv)
    return o.reshape(b, s, SWA_HQ * dh)


def _fox_attention(q, k, v, zf, b_f):
    b, s, h, dh = q.shape
    nb = s // BLK
    qh = q.transpose(0, 2, 1, 3)
    kh = k.transpose(0, 2, 1, 3)
    vh = v.transpose(0, 2, 1, 3)
    logf = jax.nn.log_sigmoid(zf.astype(jnp.float32) + b_f.astype(jnp.float32))
    c = jnp.cumsum(logf.transpose(0, 2, 1), axis=-1)
    key_pos = jnp.arange(s)

    def one_block(i):
        start = i * BLK
        qs = lax.dynamic_slice_in_dim(qh, start, BLK, axis=2)
        cs = lax.dynamic_slice_in_dim(c, start, BLK, axis=2)
        sc = jnp.einsum("bhqd,bhkd->bhqk", qs, kh).astype(jnp.float32) / math.sqrt(dh)
        sc = sc + cs[..., :, None] - c[..., None, :]
        tpos = start + jnp.arange(BLK)
        sc = jnp.where(tpos[:, None] >= key_pos[None, :], sc, NEG)
        p = jax.nn.softmax(sc, axis=-1)
        return jnp.einsum("bhqk,bhkd->bhqd", p.astype(vh.dtype), vh)

    o = lax.map(one_block, jnp.arange(nb))
    return o.transpose(1, 0, 3, 2, 4).reshape(b, s, h * dh)


def _mixer_even(h, w_in, conv_w, conv_b, ln_g, ln_b, sinks, w_out):
    b, s, _ = h.shape
    z = h @ w_in
    o0 = CONV_CH
    o1 = o0 + CONV_CH
    o2 = o1 + SWA_HQ * HEAD_DIM
    o3 = o2 + SWA_HKV * HEAD_DIM
    u, gate = z[..., :o0], z[..., o0:o1]
    q = z[..., o1:o2].reshape(b, s, SWA_HQ, HEAD_DIM)
    k = z[..., o2:o3].reshape(b, s, SWA_HKV, HEAD_DIM)
    v = z[..., o3:].reshape(b, s, SWA_HKV, HEAD_DIM)
    a_out = _conv_module(u, gate, conv_w, conv_b, ln_g, ln_b)
    b_out = _swa_sinks(q, k, v, sinks)
    return jnp.concatenate([a_out, b_out], axis=-1) @ w_out


def _mixer_odd(h, w_in, b_f, w_out):
    b, s, _ = h.shape
    z = h @ w_in
    hd = FOX_H * HEAD_DIM
    q = z[..., :hd].reshape(b, s, FOX_H, HEAD_DIM)
    k = z[..., hd:2 * hd].reshape(b, s, FOX_H, HEAD_DIM)
    v = z[..., 2 * hd:3 * hd].reshape(b, s, FOX_H, HEAD_DIM)
    zf = z[..., 3 * hd:]
    return _fox_attention(q, k, v, zf, b_f) @ w_out


def setup_inputs(seed: int = 0) -> dict:
    key = jax.random.key(seed)
    ks = jax.random.split(key, 16)
    n_even = (DEPTH + 1) // 2
    n_odd = DEPTH // 2
    f32 = jnp.float32
    nrm = lambda k, shp, fan: jax.random.normal(k, shp, f32) * fan ** -0.5
    return {
        "x": jax.random.normal(ks[0], (BATCH, SEQ, D_MODEL), f32),
        "norm_g": 1.0 + 0.05 * jax.random.normal(ks[1], (DEPTH, N_NORMS, D_MODEL), f32),
        "ffn_w_gate": nrm(ks[2], (DEPTH, 2, D_MODEL, D_FF), D_MODEL),
        "ffn_w_up": nrm(ks[3], (DEPTH, 2, D_MODEL, D_FF), D_MODEL),
        "ffn_w_down": nrm(ks[4], (DEPTH, 2, D_FF, D_MODEL), D_FF),
        "ab_w_in": nrm(ks[5], (n_even, D_MODEL, AB_IN), D_MODEL),
        "conv_w": nrm(ks[6], (n_even, CONV_K, CONV_CH), CONV_K),
        "conv_b": 0.02 * jax.random.normal(ks[7], (n_even, CONV_CH), f32),
        "conv_ln_g": 1.0 + 0.05 * jax.random.normal(ks[8], (n_even, CONV_CH), f32),
        "conv_ln_b": 0.02 * jax.random.normal(ks[9], (n_even, CONV_CH), f32),
        "swa_sinks": 0.5 * jax.random.normal(ks[10], (n_even, SWA_HQ), f32),
        "ab_w_out": nrm(ks[11], (n_even, AB_OUT, D_MODEL), AB_OUT),
        "fox_w_in": nrm(ks[12], (n_odd, D_MODEL, FOX_IN), D_MODEL),
        "fox_b_f": jax.random.uniform(ks[13], (n_odd, FOX_H), f32, 1.0, 6.0),
        "fox_w_out": nrm(ks[14], (n_odd, FOX_OUT, D_MODEL), FOX_OUT),
    }


def reference(x, norm_g, ffn_w_gate, ffn_w_up, ffn_w_down, ab_w_in, conv_w, conv_b,
              conv_ln_g, conv_ln_b, swa_sinks, ab_w_out, fox_w_in, fox_b_f, fox_w_out):
    for l in range(DEPTH):
        g = norm_g[l]
        h = _swiglu(_rmsnorm(x, g[0]), ffn_w_gate[l, 0], ffn_w_up[l, 0], ffn_w_down[l, 0])
        x = x + 0.5 * _rmsnorm(h, g[1])
        h = _rmsnorm(x, g[2])
        if l % 2 == 0:
            i = l // 2
            h = _mixer_even(h, ab_w_in[i], conv_w[i], conv_b[i], conv_ln_g[i], conv_ln_b[i],
                            swa_sinks[i], ab_w_out[i])
        else:
            i = l // 2
            h = _mixer_odd(h, fox_w_in[i], fox_b_f[i], fox_w_out[i])
        x = x + _rmsnorm(h, g[3])
        h = _swiglu(_rmsnorm(x, g[4]), ffn_w_gate[l, 1], ffn_w_up[l, 1], ffn_w_down[l, 1])
        x = x + 0.5 * _rmsnorm(h, g[5])
    return x
```

```python
import functools
import math

import jax
import jax.numpy as jnp
from jax import lax
from jax.experimental import pallas as pl
from jax.experimental.pallas import tpu as pltpu

F32 = jnp.float32
BF16 = jnp.bfloat16

LANES = 128
V7X_VMEM_BYTES = 64 * 1024 * 1024
MIB = 1024 * 1024

D_MODEL = 1024
D_FF = 2816
EPS = 1e-6
HEAD_DIM = 64
CONV_CH = 512
CONV_K = 31
SWA_HQ = 8
SWA_HKV = 2
SWA_G = SWA_HQ // SWA_HKV
WINDOW = 128
BLK = 128
FOX_H = 16
NEG = -1e30

FFN_ROWS = 512
FFN_CHUNK = 256
MIX_ROWS = 512
CONV_HALO = 32
CONV_ROWS = 64
FOX_ROWS = 512
FOX_TQ = 512
OUT_ROWS = 512
N_AUG = 3


def _rms(x, g):
    ms = jnp.mean(x * x, axis=-1, keepdims=True)
    return x * lax.rsqrt(ms + EPS) * g


def _dot(a, b):
    return jnp.dot(a, b, preferred_element_type=F32)


def _dot_nt(a, b):
    return lax.dot_general(a, b, (((1,), (1,)), ((), ())), preferred_element_type=F32)


def _split_bf16(x):
    p1 = x.astype(BF16)
    r1 = x - p1.astype(F32)
    p2 = r1.astype(BF16)
    r2 = r1 - p2.astype(F32)
    return p1, p2, r2.astype(BF16)


def _const_spec(shape):
    n = len(shape)
    return pl.BlockSpec(shape, lambda *_: (0,) * n, pipeline_mode=pl.Buffered(1))


def _ffn_body(x_ref, gpre_ref, gpost_ref, wg_ref, wu_ref, wd_ref, o_ref, a_ref):
    x = x_ref[...]
    h = _rms(x, gpre_ref[...]).astype(BF16)
    tf = FFN_CHUNK
    for c in range(wg_ref.shape[1] // tf):
        cols = slice(c * tf, (c + 1) * tf)
        g = _dot(h, wg_ref[:, cols])
        u = _dot(h, wu_ref[:, cols])
        a_ref[:, cols] = (g * jax.nn.sigmoid(g) * u).astype(BF16)
    y = _dot(a_ref[...], wd_ref[...])
    o_ref[...] = x + 0.5 * _rms(y, gpost_ref[...])


def _ffn(x, g_pre, g_post, wg, wu, wd):
    n, d = x.shape
    wg = wg.astype(BF16)
    wu = wu.astype(BF16)
    wd = wd.astype(BF16)
    tm = FFN_ROWS
    row = pl.BlockSpec((tm, d), lambda i: (i, 0))
    weights = 3 * d * D_FF * 2
    tiles = 4 * tm * d * 4 + tm * D_FF * 2 + 4 * tm * FFN_CHUNK * 4 + 2 * tm * d * 4
    return pl.pallas_call(
        _ffn_body,
        grid=(n // tm,),
        in_specs=[row, _const_spec((1, d)), _const_spec((1, d)),
                  _const_spec((d, D_FF)), _const_spec((d, D_FF)), _const_spec((D_FF, d))],
        out_specs=row,
        out_shape=jax.ShapeDtypeStruct((n, d), F32),
        scratch_shapes=[pltpu.VMEM((tm, D_FF), BF16)],
        compiler_params=pltpu.CompilerParams(
            dimension_semantics=("parallel",),
            vmem_limit_bytes=min(V7X_VMEM_BYTES - 8 * MIB, weights + 2 * tiles)),
        name="ffn",
    )(x, g_pre.reshape(1, d), g_post.reshape(1, d), wg, wu, wd)


def _even_body(sinks_ref, x_ref, g2_ref, g3_ref, win_ref, cw_ref, cb_ref, lng_ref, lnb_ref,
               bias_ref, wout_ref, o_ref, a_buf, k_buf, v_buf, y_buf, cat_buf):
    i = pl.program_id(1)
    ts = x_ref.shape[0]
    x = x_ref[...]
    h = _rms(x, g2_ref[...]).astype(BF16)
    z = _dot(h, win_ref[...])
    o_gate = CONV_CH
    o_q = 2 * CONV_CH
    o_k = o_q + SWA_HQ * HEAD_DIM
    o_v = o_k + SWA_HKV * HEAD_DIM

    @pl.when(i == 0)
    def _():
        a_buf[0:CONV_HALO, :] = jnp.zeros((CONV_HALO, CONV_CH), F32)
        k_buf[0:BLK, :] = jnp.zeros((BLK, LANES), BF16)
        v_buf[0:BLK, :] = jnp.zeros((BLK, LANES), BF16)

    @pl.when(i > 0)
    def _():
        a_buf[0:CONV_HALO, :] = a_buf[ts:ts + CONV_HALO, :]
        k_buf[0:BLK, :] = k_buf[ts:ts + BLK, :]
        v_buf[0:BLK, :] = v_buf[ts:ts + BLK, :]

    a_buf[CONV_HALO:CONV_HALO + ts, :] = z[:, 0:o_gate] * jax.nn.sigmoid(z[:, o_gate:o_q])
    k_buf[BLK:BLK + ts, :] = z[:, o_k:o_v].astype(BF16)
    v_buf[BLK:BLK + ts, :] = z[:, o_v:o_v + SWA_HKV * HEAD_DIM].astype(BF16)

    first_tap = CONV_HALO - (CONV_K - 1)
    for rb in range(ts // CONV_ROWS):
        r0 = rb * CONV_ROWS
        for cb in range(CONV_CH // LANES):
            cols = slice(cb * LANES, (cb + 1) * LANES)
            acc = jnp.broadcast_to(cb_ref[:, cols], (CONV_ROWS, LANES))
            for j in range(CONV_K):
                start = r0 + first_tap + j
                acc = acc + cw_ref[j:j + 1, cols] * a_buf[start:start + CONV_ROWS, cols]
            y_buf[r0:r0 + CONV_ROWS, cols] = acc
    y = y_buf[...]
    mu = jnp.mean(y, axis=-1, keepdims=True)
    yc = y - mu
    var = jnp.mean(yc * yc, axis=-1, keepdims=True)
    yn = yc * lax.rsqrt(var + EPS) * lng_ref[...] + lnb_ref[...]
    cat_buf[:, 0:CONV_CH] = (yn * jax.nn.sigmoid(yn)).astype(BF16)

    lane = lax.broadcasted_iota(jnp.int32, (BLK, LANES), 1)
    key_is_prev = lax.broadcasted_iota(jnp.int32, (SWA_G * BLK, 2 * BLK), 1) < BLK
    for qb in range(ts // BLK):
        rows = slice(qb * BLK, (qb + 1) * BLK)
        kk = k_buf[qb * BLK:qb * BLK + 2 * BLK, :]
        vv = v_buf[qb * BLK:qb * BLK + 2 * BLK, :]
        no_prev = jnp.logical_and(i == 0, qb == 0)
        outs = []
        for hk in range(SWA_HKV):
            mine = (lane < HEAD_DIM) if hk == 0 else (lane >= HEAD_DIM)
            qs = jnp.concatenate(
                [jnp.where(mine, z[rows, o_q + j * LANES:o_q + (j + 1) * LANES], 0.0)
                 for j in range(SWA_G)], axis=0).astype(BF16)
            s = _dot_nt(qs, kk)
            bias = jnp.where(jnp.logical_and(no_prev, key_is_prev), NEG, bias_ref[hk])
            s = s + bias
            sink = jnp.concatenate(
                [jnp.full((BLK, 1), sinks_ref[SWA_G * hk + j], F32) for j in range(SWA_G)], axis=0)
            m = jnp.maximum(jnp.max(s, axis=-1, keepdims=True), sink)
            p = jnp.exp(s - m)
            den = jnp.sum(p, axis=-1, keepdims=True) + jnp.exp(sink - m)
            outs.append(_dot(p.astype(BF16), vv) / den)
        for j in range(SWA_G):
            blk = jnp.where(lane < HEAD_DIM, outs[0][j * BLK:(j + 1) * BLK],
                            outs[1][j * BLK:(j + 1) * BLK])
            cat_buf[rows, CONV_CH + j * LANES:CONV_CH + (j + 1) * LANES] = blk.astype(BF16)

    y2 = _dot(cat_buf[...], wout_ref[...])
    o_ref[...] = x + _rms(y2, g3_ref[...])


def _swa_bias_table():
    t_loc = jnp.arange(BLK)[:, None]
    s_loc = jnp.arange(2 * BLK)[None, :]
    dist = t_loc + BLK - s_loc
    valid = (dist >= 0) & (dist < WINDOW)
    slopes = [2.0 ** (-8.0 * (g + 1) / SWA_HQ) for g in range(SWA_HQ)]
    per_head = [jnp.where(valid, -jnp.float32(sl) * dist.astype(F32), NEG) for sl in slopes]
    return jnp.stack([jnp.concatenate(per_head[SWA_G * hk:SWA_G * (hk + 1)], axis=0)
                      for hk in range(SWA_HKV)]).astype(F32)


def _pair_heads(w, axis):
    shape = w.shape
    w = jnp.moveaxis(w, axis, 0).reshape((SWA_HKV, SWA_G, HEAD_DIM) + shape[:axis] + shape[axis + 1:])
    w = jnp.swapaxes(w, 0, 1).reshape((SWA_HQ * HEAD_DIM,) + shape[:axis] + shape[axis + 1:])
    return jnp.moveaxis(w, 0, axis)


def _even_mixer(x, batch, g2, g3, w_in, conv_w, conv_b, ln_g, ln_b, sinks, w_out):
    n, d = x.shape
    seq = n // batch
    ts = MIX_ROWS
    nt = seq // ts
    o_q = 2 * CONV_CH
    o_k = o_q + SWA_HQ * HEAD_DIM
    ab_in = w_in.shape[1]
    wq = _pair_heads(w_in[:, o_q:o_k], 1) * (1.0 / math.sqrt(HEAD_DIM))
    w_in = jnp.concatenate([w_in[:, :o_q], wq, w_in[:, o_k:]], axis=1).astype(BF16)
    w_out = jnp.concatenate([w_out[:CONV_CH], _pair_heads(w_out[CONV_CH:], 0)], axis=0).astype(BF16)
    bias = _swa_bias_table()
    row = pl.BlockSpec((ts, d), lambda b, i, *_: (b * nt + i, 0))

    def const(shape):
        k = len(shape)
        return pl.BlockSpec(shape, lambda b, i, *_: (0,) * k, pipeline_mode=pl.Buffered(1))

    grid_spec = pltpu.PrefetchScalarGridSpec(
        num_scalar_prefetch=1,
        grid=(batch, nt),
        in_specs=[row, const((1, d)), const((1, d)), const((d, ab_in)), const((CONV_K, CONV_CH)),
                  const((1, CONV_CH)), const((1, CONV_CH)), const((1, CONV_CH)),
                  const(bias.shape), const((d, d))],
        out_specs=row,
        scratch_shapes=[pltpu.VMEM((CONV_HALO + ts, CONV_CH), F32),
                        pltpu.VMEM((BLK + ts, LANES), BF16),
                        pltpu.VMEM((BLK + ts, LANES), BF16),
                        pltpu.VMEM((ts, CONV_CH), F32),
                        pltpu.VMEM((ts, d), BF16)])
    return pl.pallas_call(
        _even_body,
        grid_spec=grid_spec,
        out_shape=jax.ShapeDtypeStruct((n, d), F32),
        compiler_params=pltpu.CompilerParams(
            dimension_semantics=("parallel", "arbitrary"),
            vmem_limit_bytes=48 * MIB),
        name="even_mixer",
    )(sinks, x, g2.reshape(1, d), g3.reshape(1, d), w_in, conv_w, conv_b.reshape(1, -1),
      ln_g.reshape(1, -1), ln_b.reshape(1, -1), bias, w_out)


def _fox_proj_body(x_ref, g_ref, wq_ref, wk_ref, wv_ref, wf_ref, bf_ref, tri_ref, eq_ref, ek_ref,
                   q_ref, k_ref, v_ref, carry_ref):
    i = pl.program_id(1)
    ts = x_ref.shape[0]

    @pl.when(i == 0)
    def _():
        carry_ref[...] = jnp.zeros(carry_ref.shape, F32)

    h = _rms(x_ref[...], g_ref[...]).astype(BF16)

    t = _dot(h, wf_ref[...]) + bf_ref[...]
    logf = jnp.minimum(t, 0.0) - jnp.log1p(jnp.exp(-jnp.abs(t)))
    tri = tri_ref[...]
    c = carry_ref[0:1, :]
    for piece in _split_bf16(logf):
        c = c + _dot(tri, piece)
    carry_ref[0:1, :] = c[ts - 1:ts, :]

    c1, c2, c3 = _split_bf16(c)
    lane = lax.broadcasted_iota(jnp.int32, (ts, LANES), 1)
    cp = jnp.where(lane < FOX_H, c1,
                   jnp.where(lane < 2 * FOX_H, c2,
                             jnp.where(lane < 3 * FOX_H, c3,
                                       jnp.where(lane < 4 * FOX_H, 1.0, 0.0).astype(BF16))))
    aq = _dot(cp, eq_ref[...])
    ak = _dot(cp, ek_ref[...])
    zq = _dot(h, wq_ref[...])
    zk = _dot(h, wk_ref[...])
    low = lane < HEAD_DIM
    for p in range(FOX_H // 2):
        pair = slice(p * LANES, (p + 1) * LANES)
        even = slice(2 * p * LANES, (2 * p + 1) * LANES)
        odd = slice((2 * p + 1) * LANES, (2 * p + 2) * LANES)
        q_ref[:, even] = jnp.where(low, zq[:, pair], aq[:, even]).astype(BF16)
        q_ref[:, odd] = jnp.where(low, aq[:, odd], zq[:, pair]).astype(BF16)
        k_ref[:, even] = jnp.where(low, zk[:, pair], ak[:, even]).astype(BF16)
        k_ref[:, odd] = jnp.where(low, ak[:, odd], zk[:, pair]).astype(BF16)
    v_ref[...] = _dot(h, wv_ref[...]).astype(BF16)


def _fox_placement():
    eq = [[0.0] * (FOX_H * LANES) for _ in range(LANES)]
    ek = [[0.0] * (FOX_H * LANES) for _ in range(LANES)]
    for h in range(FOX_H):
        base = h * LANES + (HEAD_DIM if h % 2 == 0 else 0)
        for a in range(N_AUG):
            eq[a * FOX_H + h][base + a] = 1.0
            eq[N_AUG * FOX_H + h][base + N_AUG + a] = 1.0
            ek[N_AUG * FOX_H + h][base + a] = 1.0
            ek[a * FOX_H + h][base + N_AUG + a] = -1.0
    return jnp.asarray(eq, BF16), jnp.asarray(ek, BF16)


def _fox_proj(x, batch, g, w_in, b_f):
    n, d = x.shape
    seq = n // batch
    ts = FOX_ROWS
    nt = seq // ts
    hd = FOX_H * HEAD_DIM
    wq = (w_in[:, :hd] * (1.0 / math.sqrt(HEAD_DIM))).astype(BF16)
    wk = w_in[:, hd:2 * hd].astype(BF16)
    wv = w_in[:, 2 * hd:3 * hd].astype(BF16)
    reps = N_AUG
    wf = jnp.pad(jnp.tile(w_in[:, 3 * hd:], (1, reps)), ((0, 0), (0, LANES - reps * FOX_H))).astype(BF16)
    bf = jnp.pad(jnp.tile(b_f, reps), (0, LANES - reps * FOX_H)).reshape(1, LANES).astype(F32)
    tri = (jnp.arange(ts)[:, None] >= jnp.arange(ts)[None, :]).astype(BF16)
    eq, ek = _fox_placement()
    row = pl.BlockSpec((ts, d), lambda b, i: (b * nt + i, 0))
    wide = pl.BlockSpec((ts, FOX_H * LANES), lambda b, i: (b * nt + i, 0))
    return pl.pallas_call(
        _fox_proj_body,
        grid=(batch, nt),
        in_specs=[row, _const_spec((1, d)), _const_spec((d, hd)), _const_spec((d, hd)),
                  _const_spec((d, hd)), _const_spec((d, LANES)), _const_spec((1, LANES)),
                  _const_spec((ts, ts)), _const_spec(eq.shape), _const_spec(ek.shape)],
        out_specs=[wide, wide, row],
        out_shape=[jax.ShapeDtypeStruct((n, FOX_H * LANES), BF16),
                   jax.ShapeDtypeStruct((n, FOX_H * LANES), BF16),
                   jax.ShapeDtypeStruct((n, hd), BF16)],
        scratch_shapes=[pltpu.VMEM((8, LANES), F32)],
        compiler_params=pltpu.CompilerParams(
            dimension_semantics=("parallel", "arbitrary"),
            vmem_limit_bytes=48 * MIB),
        name="fox_proj",
    )(x, g.reshape(1, d), wq, wk, wv, wf, bf, tri, eq, ek)


def _fox_attn_body(q_ref, k_ref, v_ref, o_ref):
    qi = pl.program_id(2)
    tq = q_ref.shape[0]
    row = lax.broadcasted_iota(jnp.int32, (tq, tq), 0)
    col = lax.broadcasted_iota(jnp.int32, (tq, tq), 1)
    causal = row >= col
    outs = []
    for hh in range(2):
        head = slice(hh * LANES, (hh + 1) * LANES)
        q = q_ref[:, head]

        def step(j, carry, masked):
            m, l, acc = carry
            start = pl.multiple_of(j * tq, tq)
            ks = k_ref[pl.ds(start, tq), head]
            vs = v_ref[pl.ds(start, tq), :]
            s = _dot_nt(q, ks)
            if masked:
                s = jnp.where(causal, s, NEG)
            m_new = jnp.maximum(m, jnp.max(s, axis=-1, keepdims=True))
            alpha = jnp.exp(m - m_new)
            p = jnp.exp(s - m_new)
            l = alpha * l + jnp.sum(p, axis=-1, keepdims=True)
            acc = alpha * acc + _dot(p.astype(BF16), vs)
            return m_new, l, acc

        init = (jnp.full((tq, 1), NEG, F32), jnp.zeros((tq, 1), F32), jnp.zeros((tq, LANES), F32))
        carry = lax.fori_loop(0, qi, functools.partial(step, masked=False), init)
        _, l, acc = step(qi, carry, True)
        outs.append(acc / l)
    lane = lax.broadcasted_iota(jnp.int32, (tq, LANES), 1)
    o_ref[...] = jnp.where(lane < HEAD_DIM, outs[0], outs[1]).astype(BF16)


def _fox_attn(q, k, v, batch):
    n = q.shape[0]
    seq = n // batch
    tq = FOX_TQ
    nq = seq // tq
    pairs = FOX_H // 2
    q = q.reshape(batch, seq, FOX_H * LANES)
    k = k.reshape(batch, seq, FOX_H * LANES)
    v = v.reshape(batch, seq, FOX_H * HEAD_DIM)
    out = pl.pallas_call(
        _fox_attn_body,
        grid=(batch, pairs, nq),
        in_specs=[pl.BlockSpec((None, tq, 2 * LANES), lambda b, p, i: (b, i, p)),
                  pl.BlockSpec((None, seq, 2 * LANES), lambda b, p, i: (b, 0, p)),
                  pl.BlockSpec((None, seq, LANES), lambda b, p, i: (b, 0, p))],
        out_specs=pl.BlockSpec((None, tq, LANES), lambda b, p, i: (b, i, p)),
        out_shape=jax.ShapeDtypeStruct((batch, seq, FOX_H * HEAD_DIM), BF16),
        compiler_params=pltpu.CompilerParams(
            dimension_semantics=("parallel", "parallel", "arbitrary"),
            vmem_limit_bytes=32 * MIB),
        name="fox_attn",
    )(q, k, v)
    return out.reshape(n, FOX_H * HEAD_DIM)


def _out_proj_body(o_ref, x_ref, w_ref, g_ref, y_ref):
    y = _dot(o_ref[...], w_ref[...])
    y_ref[...] = x_ref[...] + _rms(y, g_ref[...])


def _out_proj(o, x, w_out, g):
    n, d = x.shape
    tm = OUT_ROWS
    row_o = pl.BlockSpec((tm, o.shape[1]), lambda i: (i, 0))
    row_x = pl.BlockSpec((tm, d), lambda i: (i, 0))
    return pl.pallas_call(
        _out_proj_body,
        grid=(n // tm,),
        in_specs=[row_o, row_x, _const_spec(w_out.shape), _const_spec((1, d))],
        out_specs=row_x,
        out_shape=jax.ShapeDtypeStruct((n, d), F32),
        compiler_params=pltpu.CompilerParams(
            dimension_semantics=("parallel",),
            vmem_limit_bytes=32 * MIB),
        name="out_proj",
    )(o, x, w_out.astype(BF16), g.reshape(1, d))


def kernel(x, norm_g, ffn_w_gate, ffn_w_up, ffn_w_down, ab_w_in, conv_w, conv_b, conv_ln_g,
           conv_ln_b, swa_sinks, ab_w_out, fox_w_in, fox_b_f, fox_w_out):
    batch, seq, d = x.shape
    depth = norm_g.shape[0]
    xf = x.reshape(batch * seq, d)
    for l in range(depth):
        g = norm_g[l]
        xf = _ffn(xf, g[0], g[1], ffn_w_gate[l, 0], ffn_w_up[l, 0], ffn_w_down[l, 0])
        i = l // 2
        if l % 2 == 0:
            xf = _even_mixer(xf, batch, g[2], g[3], ab_w_in[i], conv_w[i], conv_b[i],
                             conv_ln_g[i], conv_ln_b[i], swa_sinks[i], ab_w_out[i])
        else:
            q, k, v = _fox_proj(xf, batch, g[2], fox_w_in[i], fox_b_f[i])
            o = _fox_attn(q, k, v, batch)
            xf = _out_proj(o, xf, fox_w_out[i], g[3])
        xf = _ffn(xf, g[4], g[5], ffn_w_gate[l, 1], ffn_w_up[l, 1], ffn_w_down[l, 1])
    return xf.reshape(batch, seq, d)
```

```python
import functools
import math

import jax
import jax.numpy as jnp
from jax import lax
from jax.experimental import pallas as pl
from jax.experimental.pallas import tpu as pltpu

F32 = jnp.float32
BF16 = jnp.bfloat16

LANES = 128
V7X_VMEM_BYTES = 64 * 1024 * 1024
MIB = 1024 * 1024

D_MODEL = 1024
D_FF = 2816
EPS = 1e-6
HEAD_DIM = 64
CONV_CH = 512
CONV_K = 31
SWA_HQ = 8
SWA_HKV = 2
SWA_G = SWA_HQ // SWA_HKV
WINDOW = 128
BLK = 128
FOX_H = 16
NEG = -1e30

FFN_ROWS = 512
FFN_CHUNK = 256
MIX_ROWS = 512
CONV_HALO = 32
CONV_PHASES = 4
FOX_ROWS = 512
FOX_TQ = 512
N_AUG = 3
LOG2E = 1.4426950408889634


def _rms(x, g):
    ms = jnp.mean(x * x, axis=-1, keepdims=True)
    return x * lax.rsqrt(ms + EPS) * g


def _dot(a, b):
    return jnp.dot(a, b, preferred_element_type=F32)


def _dot_nt(a, b):
    return lax.dot_general(a, b, (((1,), (1,)), ((), ())), preferred_element_type=F32)


def _split_bf16(x):
    p1 = x.astype(BF16)
    r1 = x - p1.astype(F32)
    p2 = r1.astype(BF16)
    r2 = r1 - p2.astype(F32)
    return p1, p2, r2.astype(BF16)


def _const_spec(shape):
    n = len(shape)
    return pl.BlockSpec(shape, lambda *_: (0,) * n, pipeline_mode=pl.Buffered(1))


def _ffn_body(*refs, mix_in):
    if mix_in:
        attn_ref, wmix_ref, gmix_ref, *refs = refs
    x_ref, gpre_ref, gpost_ref, wg_ref, wu_ref, wd_ref, o_ref, a_ref = refs
    x = x_ref[...]
    if mix_in:
        x = x + _rms(_dot(attn_ref[...], wmix_ref[...]), gmix_ref[...])
    h = _rms(x, gpre_ref[...]).astype(BF16)
    tf = FFN_CHUNK
    for c in range(wg_ref.shape[1] // tf):
        cols = slice(c * tf, (c + 1) * tf)
        g = _dot(h, wg_ref[:, cols])
        u = _dot(h, wu_ref[:, cols])
        a_ref[:, cols] = (g * jax.nn.sigmoid(g) * u).astype(BF16)
    y = _dot(a_ref[...], wd_ref[...])
    o_ref[...] = x + 0.5 * _rms(y, gpost_ref[...])


def _ffn(x, g_pre, g_post, wg, wu, wd, mix=None):
    n, d = x.shape
    tm = FFN_ROWS
    row = pl.BlockSpec((tm, d), lambda i: (i, 0))
    args = [x, g_pre.reshape(1, d), g_post.reshape(1, d),
            wg.astype(BF16), wu.astype(BF16), wd.astype(BF16)]
    in_specs = [row, _const_spec((1, d)), _const_spec((1, d)),
                _const_spec((d, D_FF)), _const_spec((d, D_FF)), _const_spec((D_FF, d))]
    weights = 3 * d * D_FF * 2
    tiles = 4 * tm * d * 4 + tm * D_FF * 2 + 4 * tm * FFN_CHUNK * 4 + 2 * tm * d * 4
    if mix is not None:
        attn, w_mix, g_mix = mix
        args = [attn, w_mix.astype(BF16), g_mix.reshape(1, d)] + args
        in_specs = [row, _const_spec((d, d)), _const_spec((1, d))] + in_specs
        weights += d * d * 2
        tiles += 2 * tm * d * 2 + tm * d * 4
    return pl.pallas_call(
        functools.partial(_ffn_body, mix_in=mix is not None),
        grid=(n // tm,),
        in_specs=in_specs,
        out_specs=row,
        out_shape=jax.ShapeDtypeStruct((n, d), F32),
        scratch_shapes=[pltpu.VMEM((tm, D_FF), BF16)],
        compiler_params=pltpu.CompilerParams(
            dimension_semantics=("parallel",),
            vmem_limit_bytes=min(V7X_VMEM_BYTES - 8 * MIB, weights + 2 * tiles)),
        name="ffn_mix" if mix is not None else "ffn",
    )(*args)


def _even_body(sinks_ref, x_ref, g2_ref, g3_ref, win_ref, cw_ref, cb_ref, lng_ref, lnb_ref,
               bias_ref, wout_ref, o_ref, a_buf, k_buf, v_buf, y_buf, cat_buf):
    i = pl.program_id(1)
    ts = x_ref.shape[0]
    x = x_ref[...]
    h = _rms(x, g2_ref[...]).astype(BF16)
    z = _dot(h, win_ref[...])
    o_gate = CONV_CH
    o_q = 2 * CONV_CH
    o_k = o_q + SWA_HQ * HEAD_DIM
    o_v = o_k + SWA_HKV * HEAD_DIM

    n_slabs = CONV_CH // LANES

    @pl.when(i == 0)
    def _():
        a_buf[:, 0:CONV_HALO, :] = jnp.zeros((n_slabs, CONV_HALO, LANES), F32)
        k_buf[0:BLK, :] = jnp.zeros((BLK, LANES), BF16)
        v_buf[0:BLK, :] = jnp.zeros((BLK, LANES), BF16)

    @pl.when(i > 0)
    def _():
        a_buf[:, 0:CONV_HALO, :] = a_buf[:, ts:ts + CONV_HALO, :]
        k_buf[0:BLK, :] = k_buf[ts:ts + BLK, :]
        v_buf[0:BLK, :] = v_buf[ts:ts + BLK, :]

    a = z[:, 0:o_gate] * jax.nn.sigmoid(z[:, o_gate:o_q])
    for cb in range(n_slabs):
        a_buf[cb, CONV_HALO:CONV_HALO + ts, :] = a[:, cb * LANES:(cb + 1) * LANES]
    k_buf[BLK:BLK + ts, :] = z[:, o_k:o_v].astype(BF16)
    v_buf[BLK:BLK + ts, :] = z[:, o_v:o_v + SWA_HKV * HEAD_DIM].astype(BF16)

    first_tap = CONV_HALO - (CONV_K - 1)
    n_rows = ts // CONV_PHASES
    for cb in range(n_slabs):
        cols = slice(cb * LANES, (cb + 1) * LANES)
        bias = jnp.broadcast_to(cb_ref[:, cols], (n_rows, LANES))
        for r in range(CONV_PHASES):
            acc = bias
            for j in range(CONV_K):
                taps = a_buf[cb, pl.ds(first_tap + r + j, n_rows, stride=CONV_PHASES), :]
                acc = acc + cw_ref[j:j + 1, cols] * taps
            y_buf[cb, pl.ds(r, n_rows, stride=CONV_PHASES), :] = acc
    ys = [y_buf[cb] for cb in range(n_slabs)]
    mu = sum(jnp.sum(y, axis=-1, keepdims=True) for y in ys) * (1.0 / CONV_CH)
    ycs = [y - mu for y in ys]
    var = sum(jnp.sum(yc * yc, axis=-1, keepdims=True) for yc in ycs) * (1.0 / CONV_CH)
    inv = lax.rsqrt(var + EPS)
    for cb in range(n_slabs):
        cols = slice(cb * LANES, (cb + 1) * LANES)
        yn = ycs[cb] * inv * lng_ref[:, cols] + lnb_ref[:, cols]
        cat_buf[:, cols] = (yn * jax.nn.sigmoid(yn)).astype(BF16)

    lane = lax.broadcasted_iota(jnp.int32, (BLK, LANES), 1)
    key_is_prev = lax.broadcasted_iota(jnp.int32, (SWA_G * BLK, 2 * BLK), 1) < BLK
    for qb in range(ts // BLK):
        rows = slice(qb * BLK, (qb + 1) * BLK)
        kk = k_buf[qb * BLK:qb * BLK + 2 * BLK, :]
        vv = v_buf[qb * BLK:qb * BLK + 2 * BLK, :]
        no_prev = jnp.logical_and(i == 0, qb == 0)
        outs = []
        for hk in range(SWA_HKV):
            mine = (lane < HEAD_DIM) if hk == 0 else (lane >= HEAD_DIM)
            qs = jnp.concatenate(
                [jnp.where(mine, z[rows, o_q + j * LANES:o_q + (j + 1) * LANES], 0.0)
                 for j in range(SWA_G)], axis=0).astype(BF16)
            s = _dot_nt(qs, kk)
            bias = jnp.where(jnp.logical_and(no_prev, key_is_prev), NEG, bias_ref[hk])
            s = s + bias
            sink = jnp.concatenate(
                [jnp.full((BLK, 1), sinks_ref[SWA_G * hk + j], F32) for j in range(SWA_G)], axis=0)
            m = jnp.maximum(jnp.max(s, axis=-1, keepdims=True), sink)
            p = jnp.exp(s - m)
            den = jnp.sum(p, axis=-1, keepdims=True) + jnp.exp(sink - m)
            outs.append(_dot(p.astype(BF16), vv) / den)
        for j in range(SWA_G):
            blk = jnp.where(lane < HEAD_DIM, outs[0][j * BLK:(j + 1) * BLK],
                            outs[1][j * BLK:(j + 1) * BLK])
            cat_buf[rows, CONV_CH + j * LANES:CONV_CH + (j + 1) * LANES] = blk.astype(BF16)

    y2 = _dot(cat_buf[...], wout_ref[...])
    o_ref[...] = x + _rms(y2, g3_ref[...])


def _swa_bias_table():
    t_loc = jnp.arange(BLK)[:, None]
    s_loc = jnp.arange(2 * BLK)[None, :]
    dist = t_loc + BLK - s_loc
    valid = (dist >= 0) & (dist < WINDOW)
    slopes = [2.0 ** (-8.0 * (g + 1) / SWA_HQ) for g in range(SWA_HQ)]
    per_head = [jnp.where(valid, -jnp.float32(sl) * dist.astype(F32), NEG) for sl in slopes]
    return jnp.stack([jnp.concatenate(per_head[SWA_G * hk:SWA_G * (hk + 1)], axis=0)
                      for hk in range(SWA_HKV)]).astype(F32)


def _pair_heads(w, axis):
    shape = w.shape
    w = jnp.moveaxis(w, axis, 0).reshape((SWA_HKV, SWA_G, HEAD_DIM) + shape[:axis] + shape[axis + 1:])
    w = jnp.swapaxes(w, 0, 1).reshape((SWA_HQ * HEAD_DIM,) + shape[:axis] + shape[axis + 1:])
    return jnp.moveaxis(w, 0, axis)


def _even_mixer(x, batch, g2, g3, w_in, conv_w, conv_b, ln_g, ln_b, sinks, w_out):
    n, d = x.shape
    seq = n // batch
    ts = MIX_ROWS
    nt = seq // ts
    o_q = 2 * CONV_CH
    o_k = o_q + SWA_HQ * HEAD_DIM
    ab_in = w_in.shape[1]
    wq = _pair_heads(w_in[:, o_q:o_k], 1) * (1.0 / math.sqrt(HEAD_DIM))
    w_in = jnp.concatenate([w_in[:, :o_q], wq, w_in[:, o_k:]], axis=1).astype(BF16)
    w_out = jnp.concatenate([w_out[:CONV_CH], _pair_heads(w_out[CONV_CH:], 0)], axis=0).astype(BF16)
    bias = _swa_bias_table()
    row = pl.BlockSpec((ts, d), lambda b, i, *_: (b * nt + i, 0))

    def const(shape):
        k = len(shape)
        return pl.BlockSpec(shape, lambda b, i, *_: (0,) * k, pipeline_mode=pl.Buffered(1))

    grid_spec = pltpu.PrefetchScalarGridSpec(
        num_scalar_prefetch=1,
        grid=(batch, nt),
        in_specs=[row, const((1, d)), const((1, d)), const((d, ab_in)), const((CONV_K, CONV_CH)),
                  const((1, CONV_CH)), const((1, CONV_CH)), const((1, CONV_CH)),
                  const(bias.shape), const((d, d))],
        out_specs=row,
        scratch_shapes=[pltpu.VMEM((CONV_CH // LANES, CONV_HALO + ts, LANES), F32),
                        pltpu.VMEM((BLK + ts, LANES), BF16),
                        pltpu.VMEM((BLK + ts, LANES), BF16),
                        pltpu.VMEM((CONV_CH // LANES, ts, LANES), F32),
                        pltpu.VMEM((ts, d), BF16)])
    return pl.pallas_call(
        _even_body,
        grid_spec=grid_spec,
        out_shape=jax.ShapeDtypeStruct((n, d), F32),
        compiler_params=pltpu.CompilerParams(
            dimension_semantics=("parallel", "arbitrary"),
            vmem_limit_bytes=48 * MIB),
        name="even_mixer",
    )(sinks, x, g2.reshape(1, d), g3.reshape(1, d), w_in, conv_w, conv_b.reshape(1, -1),
      ln_g.reshape(1, -1), ln_b.reshape(1, -1), bias, w_out)


def _fox_proj_body(x_ref, g_ref, wq_ref, wk_ref, wv_ref, wf_ref, bf_ref, tri_ref, eq_ref, ek_ref,
                   q_ref, k_ref, v_ref, carry_ref):
    i = pl.program_id(1)
    ts = x_ref.shape[0]

    @pl.when(i == 0)
    def _():
        carry_ref[...] = jnp.zeros(carry_ref.shape, F32)

    h = _rms(x_ref[...], g_ref[...]).astype(BF16)

    t = _dot(h, wf_ref[...]) + bf_ref[...]
    logf = jnp.minimum(t, 0.0) - jnp.log1p(jnp.exp(-jnp.abs(t)))
    tri = tri_ref[...]
    c = carry_ref[0:1, :]
    for piece in _split_bf16(logf):
        c = c + _dot(tri, piece)
    carry_ref[0:1, :] = c[ts - 1:ts, :]

    c1, c2, c3 = _split_bf16(c * LOG2E)
    lane = lax.broadcasted_iota(jnp.int32, (ts, LANES), 1)
    cp = jnp.where(lane < FOX_H, c1,
                   jnp.where(lane < 2 * FOX_H, c2,
                             jnp.where(lane < 3 * FOX_H, c3,
                                       jnp.where(lane < 4 * FOX_H, 1.0, 0.0).astype(BF16))))
    aq = _dot(cp, eq_ref[...])
    ak = _dot(cp, ek_ref[...])
    zq = _dot(h, wq_ref[...])
    zk = _dot(h, wk_ref[...])
    zv = _dot(h, wv_ref[...])
    low = lane < HEAD_DIM
    one_hi = (lane == HEAD_DIM).astype(F32)
    one_lo = (lane == 0).astype(F32)
    for p in range(FOX_H // 2):
        pair = slice(p * LANES, (p + 1) * LANES)
        even = slice(2 * p * LANES, (2 * p + 1) * LANES)
        odd = slice((2 * p + 1) * LANES, (2 * p + 2) * LANES)
        q_ref[:, even] = jnp.where(low, zq[:, pair], aq[:, pair]).astype(BF16)
        q_ref[:, odd] = jnp.where(low, aq[:, pair], zq[:, pair]).astype(BF16)
        k_ref[:, even] = jnp.where(low, zk[:, pair], ak[:, pair]).astype(BF16)
        k_ref[:, odd] = jnp.where(low, ak[:, pair], zk[:, pair]).astype(BF16)
        v_ref[:, even] = jnp.where(low, zv[:, pair], one_hi).astype(BF16)
        v_ref[:, odd] = jnp.where(low, one_lo, zv[:, pair]).astype(BF16)


def _fox_placement():
    eq = [[0.0] * (FOX_H // 2 * LANES) for _ in range(LANES)]
    ek = [[0.0] * (FOX_H // 2 * LANES) for _ in range(LANES)]
    for h in range(FOX_H):
        base = (h // 2) * LANES + (HEAD_DIM if h % 2 == 0 else 0)
        for a in range(N_AUG):
            eq[a * FOX_H + h][base + a] = 1.0
            eq[N_AUG * FOX_H + h][base + N_AUG + a] = 1.0
            ek[N_AUG * FOX_H + h][base + a] = 1.0
            ek[a * FOX_H + h][base + N_AUG + a] = -1.0
    return jnp.asarray(eq, BF16), jnp.asarray(ek, BF16)


def _fox_proj(x, batch, g, w_in, b_f):
    n, d = x.shape
    seq = n // batch
    ts = FOX_ROWS
    nt = seq // ts
    hd = FOX_H * HEAD_DIM
    wq = (w_in[:, :hd] * (LOG2E / math.sqrt(HEAD_DIM))).astype(BF16)
    wk = w_in[:, hd:2 * hd].astype(BF16)
    wv = w_in[:, 2 * hd:3 * hd].astype(BF16)
    reps = N_AUG
    wf = jnp.pad(jnp.tile(w_in[:, 3 * hd:], (1, reps)), ((0, 0), (0, LANES - reps * FOX_H))).astype(BF16)
    bf = jnp.pad(jnp.tile(b_f, reps), (0, LANES - reps * FOX_H)).reshape(1, LANES).astype(F32)
    tri = (jnp.arange(ts)[:, None] >= jnp.arange(ts)[None, :]).astype(BF16)
    eq, ek = _fox_placement()
    row = pl.BlockSpec((ts, d), lambda b, i: (b * nt + i, 0))
    wide = pl.BlockSpec((ts, FOX_H * LANES), lambda b, i: (b * nt + i, 0))
    return pl.pallas_call(
        _fox_proj_body,
        grid=(batch, nt),
        in_specs=[row, _const_spec((1, d)), _const_spec((d, hd)), _const_spec((d, hd)),
                  _const_spec((d, hd)), _const_spec((d, LANES)), _const_spec((1, LANES)),
                  _const_spec((ts, ts)), _const_spec(eq.shape), _const_spec(ek.shape)],
        out_specs=[wide, wide, wide],
        out_shape=[jax.ShapeDtypeStruct((n, FOX_H * LANES), BF16)] * 3,
        scratch_shapes=[pltpu.VMEM((8, LANES), F32)],
        compiler_params=pltpu.CompilerParams(
            dimension_semantics=("parallel", "arbitrary"),
            vmem_limit_bytes=48 * MIB),
        name="fox_proj",
    )(x, g.reshape(1, d), wq, wk, wv, wf, bf, tri, eq, ek)


def _fox_attn_body(q_ref, k_ref, v_ref, o_ref):
    seq = q_ref.shape[0]
    tq = FOX_TQ
    heads =[slice(hh * LANES, (hh + 1) * LANES) for hh in range(2)]

    def advance(state, rows, keys, diagonal):
        nr, nk = rows.stop - rows.start, keys.stop - keys.start
        new = []
        for (m, acc), head in zip(state, heads):
            s = _dot_nt(q_ref[rows, head], k_ref[keys, head])
            if diagonal:
                row = lax.broadcasted_iota(jnp.int32, (nr, nk), 0)
                col = lax.broadcasted_iota(jnp.int32, (nr, nk), 1)
                s = jnp.where(col - (nk - nr) <= row, s, NEG)
            m_new = jnp.maximum(m, jnp.max(s, axis=-1, keepdims=True))
            p = jnp.exp2(s - m_new).astype(BF16)
            acc = jnp.exp2(m - m_new) * acc + _dot(p, v_ref[keys, head])
            new.append((m_new, acc))
        return new

    def finish(state, rows):
        (_, acc0), (_, acc1) = state
        lane = lax.broadcasted_iota(jnp.int32, acc0.shape, 1)
        out0 = acc0 / acc0[:, HEAD_DIM:HEAD_DIM + 1]
        out1 = acc1 / acc1[:, 0:1]
        o_ref[rows, :] = jnp.where(lane < HEAD_DIM, out0, out1).astype(BF16)

    for qi in range(seq // tq):
        q0 = qi * tq
        rows = slice(q0, q0 + tq)
        state = [(jnp.full((tq, 1), NEG, F32), jnp.zeros((tq, LANES), F32))] * 2
        for j in range(qi + 1):
            state = advance(state, rows, slice(j * tq, (j + 1) * tq), j == qi)
        finish(state, rows)


def _fox_attn(q, k, v, batch):
    n = q.shape[0]
    seq = n // batch
    pairs = FOX_H // 2
    q = q.reshape(batch, seq, FOX_H * LANES)
    k = k.reshape(batch, seq, FOX_H * LANES)
    v = v.reshape(batch, seq, FOX_H * LANES)
    pair_block = pl.BlockSpec((None, seq, 2 * LANES), lambda b, p: (b, 0, p))
    out = pl.pallas_call(
        _fox_attn_body,
        grid=(batch, pairs),
        in_specs=[pair_block, pair_block, pair_block],
        out_specs=pl.BlockSpec((None, seq, LANES), lambda b, p: (b, 0, p)),
        out_shape=jax.ShapeDtypeStruct((batch, seq, FOX_H * HEAD_DIM), BF16),
        compiler_params=pltpu.CompilerParams(
            dimension_semantics=("parallel", "parallel"),
            vmem_limit_bytes=32 * MIB),
        name="fox_attn",
    )(q, k, v)
    return out.reshape(n, FOX_H * HEAD_DIM)


def kernel(x, norm_g, ffn_w_gate, ffn_w_up, ffn_w_down, ab_w_in, conv_w, conv_b, conv_ln_g,
           conv_ln_b, swa_sinks, ab_w_out, fox_w_in, fox_b_f, fox_w_out):
    batch, seq, d = x.shape
    depth = norm_g.shape[0]
    xf = x.reshape(batch * seq, d)
    for l in range(depth):
        g = norm_g[l]
        xf = _ffn(xf, g[0], g[1], ffn_w_gate[l, 0], ffn_w_up[l, 0], ffn_w_down[l, 0])
        i = l // 2
        mix = None
        if l % 2 == 0:
            xf = _even_mixer(xf, batch, g[2], g[3], ab_w_in[i], conv_w[i], conv_b[i],
                             conv_ln_g[i], conv_ln_b[i], swa_sinks[i], ab_w_out[i])
        else:
            q, k, v = _fox_proj(xf, batch, g[2], fox_w_in[i], fox_b_f[i])
            mix = (_fox_attn(q, k, v, batch), fox_w_out[i], g[3])
        xf = _ffn(xf, g[4], g[5], ffn_w_gate[l, 1], ffn_w_up[l, 1], ffn_w_down[l, 1], mix)
    return xf.reshape(batch, seq, d)
```

```python
import functools
import math

import jax
import jax.numpy as jnp
from jax import lax
from jax.experimental import pallas as pl
from jax.experimental.pallas import tpu as pltpu

F32 = jnp.float32
BF16 = jnp.bfloat16

LANES = 128
V7X_VMEM_BYTES = 64 * 1024 * 1024
MIB = 1024 * 1024

D_MODEL = 1024
D_FF = 2816
EPS = 1e-6
HEAD_DIM = 64
CONV_CH = 512
CONV_K = 31
SWA_HQ = 8
SWA_HKV = 2
SWA_G = SWA_HQ // SWA_HKV
WINDOW = 128
BLK = 128
FOX_H = 16
NEG = -1e30

FFN_ROWS = 1024
FFN_SUB_ROWS = 512
FFN_CHUNK = 256
MIX_ROWS = 512
CONV_HALO = 32
CONV_PHASES = 4
FOX_ROWS = 512
FOX_TQ = 512
N_AUG = 3
LOG2E = 1.4426950408889634


def _rms(x, g):
    ms = jnp.mean(x * x, axis=-1, keepdims=True)
    return x * lax.rsqrt(ms + EPS) * g


def _dot(a, b):
    return jnp.dot(a, b, preferred_element_type=F32)


def _dot_nt(a, b):
    return lax.dot_general(a, b, (((1,), (1,)), ((), ())), preferred_element_type=F32)


def _split_bf16(x):
    p1 = x.astype(BF16)
    r1 = x - p1.astype(F32)
    p2 = r1.astype(BF16)
    r2 = r1 - p2.astype(F32)
    return p1, p2, r2.astype(BF16)


def _const_spec(shape):
    n = len(shape)
    return pl.BlockSpec(shape, lambda *_: (0,) * n, pipeline_mode=pl.Buffered(1))


def _ffn_body(*refs, mix_in):
    if mix_in:
        attn_ref, wmix_ref, gmix_ref, *refs = refs
    x_ref, gpre_ref, gpost_ref, wg_ref, wu_ref, wd_ref, o_ref, a_ref = refs
    tf = FFN_CHUNK
    for r0 in range(0, x_ref.shape[0], FFN_SUB_ROWS):
        rows = slice(r0, r0 + FFN_SUB_ROWS)
        x = x_ref[rows, :]
        if mix_in:
            x = x + _rms(_dot(attn_ref[rows, :], wmix_ref[...]), gmix_ref[...])
        h = _rms(x, gpre_ref[...]).astype(BF16)
        for c in range(wg_ref.shape[1] // tf):
            cols = slice(c * tf, (c + 1) * tf)
            g = _dot(h, wg_ref[:, cols])
            u = _dot(h, wu_ref[:, cols])
            a_ref[rows, cols] = (g * jax.nn.sigmoid(g) * u).astype(BF16)
        y = _dot(a_ref[rows, :], wd_ref[...])
        o_ref[rows, :] = x + 0.5 * _rms(y, gpost_ref[...])


def _ffn(x, g_pre, g_post, wg, wu, wd, which, mix=None):
    n, d = x.shape
    tm = FFN_ROWS
    row = pl.BlockSpec((tm, d), lambda i: (i, 0))

    def stacked(rows, cols):
        return pl.BlockSpec((None, None, rows, cols), lambda i: which + (0, 0),
                            pipeline_mode=pl.Buffered(1))

    args = [x, g_pre.reshape(1, d), g_post.reshape(1, d), wg, wu, wd]
    in_specs = [row, _const_spec((1, d)), _const_spec((1, d)),
                stacked(d, D_FF), stacked(d, D_FF), stacked(D_FF, d)]
    weights = 3 * d * D_FF * 2
    tiles = 4 * tm * d * 4 + tm * D_FF * 2 + 4 * tm * FFN_CHUNK * 4 + 2 * tm * d * 4
    if mix is not None:
        attn, w_mix, g_mix = mix
        args = [attn, w_mix.astype(BF16), g_mix.reshape(1, d)] + args
        in_specs = [row, _const_spec((d, d)), _const_spec((1, d))] + in_specs
        weights += d * d * 2
        tiles += 2 * tm * d * 2 + tm * d * 4
    return pl.pallas_call(
        functools.partial(_ffn_body, mix_in=mix is not None),
        grid=(n // tm,),
        in_specs=in_specs,
        out_specs=row,
        out_shape=jax.ShapeDtypeStruct((n, d), F32),
        scratch_shapes=[pltpu.VMEM((tm, D_FF), BF16)],
        compiler_params=pltpu.CompilerParams(
            dimension_semantics=("parallel",),
            vmem_limit_bytes=min(V7X_VMEM_BYTES - 8 * MIB, weights + 2 * tiles)),
        name="ffn_mix" if mix is not None else "ffn",
    )(*args)


def _even_body(sinks_ref, x_ref, g2_ref, g3_ref, win_ref, cw_ref, cb_ref, lng_ref, lnb_ref,
               bias_ref, wout_ref, o_ref, a_buf, k_buf, v_buf, y_buf, cat_buf):
    i = pl.program_id(1)
    ts = x_ref.shape[0]
    x = x_ref[...]
    h = _rms(x, g2_ref[...]).astype(BF16)
    z = _dot(h, win_ref[...])
    o_gate = CONV_CH
    o_q = 2 * CONV_CH
    o_k = o_q + SWA_HQ * HEAD_DIM
    o_v = o_k + SWA_HKV * HEAD_DIM

    n_slabs = CONV_CH // LANES

    @pl.when(i == 0)
    def _():
        a_buf[:, 0:CONV_HALO, :] = jnp.zeros((n_slabs, CONV_HALO, LANES), F32)
        k_buf[0:BLK, :] = jnp.zeros((BLK, LANES), BF16)
        v_buf[0:BLK, :] = jnp.zeros((BLK, LANES), BF16)

    @pl.when(i > 0)
    def _():
        a_buf[:, 0:CONV_HALO, :] = a_buf[:, ts:ts + CONV_HALO, :]
        k_buf[0:BLK, :] = k_buf[ts:ts + BLK, :]
        v_buf[0:BLK, :] = v_buf[ts:ts + BLK, :]

    a = z[:, 0:o_gate] * jax.nn.sigmoid(z[:, o_gate:o_q])
    for cb in range(n_slabs):
        a_buf[cb, CONV_HALO:CONV_HALO + ts, :] = a[:, cb * LANES:(cb + 1) * LANES]
    k_buf[BLK:BLK + ts, :] = z[:, o_k:o_v].astype(BF16)
    v_buf[BLK:BLK + ts, :] = z[:, o_v:o_v + SWA_HKV * HEAD_DIM].astype(BF16)

    first_tap = CONV_HALO - (CONV_K - 1)
    n_rows = ts // CONV_PHASES
    for cb in range(n_slabs):
        cols = slice(cb * LANES, (cb + 1) * LANES)
        bias = jnp.broadcast_to(cb_ref[:, cols], (n_rows, LANES))
        for r in range(CONV_PHASES):
            acc = bias
            for j in range(CONV_K):
                taps = a_buf[cb, pl.ds(first_tap + r + j, n_rows, stride=CONV_PHASES), :]
                acc = acc + cw_ref[j:j + 1, cols] * taps
            y_buf[cb, pl.ds(r, n_rows, stride=CONV_PHASES), :] = acc
    ys = [y_buf[cb] for cb in range(n_slabs)]
    mu = sum(jnp.sum(y, axis=-1, keepdims=True) for y in ys) * (1.0 / CONV_CH)
    ycs = [y - mu for y in ys]
    var = sum(jnp.sum(yc * yc, axis=-1, keepdims=True) for yc in ycs) * (1.0 / CONV_CH)
    inv = lax.rsqrt(var + EPS)
    for cb in range(n_slabs):
        cols = slice(cb * LANES, (cb + 1) * LANES)
        yn = ycs[cb] * inv * lng_ref[:, cols] + lnb_ref[:, cols]
        cat_buf[:, cols] = (yn * jax.nn.sigmoid(yn)).astype(BF16)

    lane = lax.broadcasted_iota(jnp.int32, (BLK, LANES), 1)
    key_is_prev = lax.broadcasted_iota(jnp.int32, (SWA_G * BLK, 2 * BLK), 1) < BLK
    for qb in range(ts // BLK):
        rows = slice(qb * BLK, (qb + 1) * BLK)
        kk = k_buf[qb * BLK:qb * BLK + 2 * BLK, :]
        vv = v_buf[qb * BLK:qb * BLK + 2 * BLK, :]
        no_prev = jnp.logical_and(i == 0, qb == 0)
        outs = []
        for hk in range(SWA_HKV):
            mine = (lane < HEAD_DIM) if hk == 0 else (lane >= HEAD_DIM)
            qs = jnp.concatenate(
                [jnp.where(mine, z[rows, o_q + j * LANES:o_q + (j + 1) * LANES], 0.0)
                 for j in range(SWA_G)], axis=0).astype(BF16)
            s = _dot_nt(qs, kk)
            bias = jnp.where(jnp.logical_and(no_prev, key_is_prev), NEG, bias_ref[hk])
            s = s + bias
            sink = jnp.concatenate(
                [jnp.full((BLK, 1), sinks_ref[SWA_G * hk + j], F32) for j in range(SWA_G)], axis=0)
            m = jnp.maximum(jnp.max(s, axis=-1, keepdims=True), sink)
            p = jnp.exp(s - m)
            den = jnp.sum(p, axis=-1, keepdims=True) + jnp.exp(sink - m)
            outs.append(_dot(p.astype(BF16), vv) / den)
        for j in range(SWA_G):
            blk = jnp.where(lane < HEAD_DIM, outs[0][j * BLK:(j + 1) * BLK],
                            outs[1][j * BLK:(j + 1) * BLK])
            cat_buf[rows, CONV_CH + j * LANES:CONV_CH + (j + 1) * LANES] = blk.astype(BF16)

    y2 = _dot(cat_buf[...], wout_ref[...])
    o_ref[...] = x + _rms(y2, g3_ref[...])


def _swa_bias_table():
    t_loc = jnp.arange(BLK)[:, None]
    s_loc = jnp.arange(2 * BLK)[None, :]
    dist = t_loc + BLK - s_loc
    valid = (dist >= 0) & (dist < WINDOW)
    slopes = [2.0 ** (-8.0 * (g + 1) / SWA_HQ) for g in range(SWA_HQ)]
    per_head = [jnp.where(valid, -jnp.float32(sl) * dist.astype(F32), NEG) for sl in slopes]
    return jnp.stack([jnp.concatenate(per_head[SWA_G * hk:SWA_G * (hk + 1)], axis=0)
                      for hk in range(SWA_HKV)]).astype(F32)


def _pair_heads(w, axis):
    shape = w.shape
    w = jnp.moveaxis(w, axis, 0).reshape((SWA_HKV, SWA_G, HEAD_DIM) + shape[:axis] + shape[axis + 1:])
    w = jnp.swapaxes(w, 0, 1).reshape((SWA_HQ * HEAD_DIM,) + shape[:axis] + shape[axis + 1:])
    return jnp.moveaxis(w, 0, axis)


def _even_mixer(x, batch, g2, g3, w_in, conv_w, conv_b, ln_g, ln_b, sinks, w_out):
    n, d = x.shape
    seq = n // batch
    ts = MIX_ROWS
    nt = seq // ts
    o_q = 2 * CONV_CH
    o_k = o_q + SWA_HQ * HEAD_DIM
    ab_in = w_in.shape[1]
    wq = _pair_heads(w_in[:, o_q:o_k], 1) * (1.0 / math.sqrt(HEAD_DIM))
    w_in = jnp.concatenate([w_in[:, :o_q], wq, w_in[:, o_k:]], axis=1).astype(BF16)
    w_out = jnp.concatenate([w_out[:CONV_CH], _pair_heads(w_out[CONV_CH:], 0)], axis=0).astype(BF16)
    bias = _swa_bias_table()
    row = pl.BlockSpec((ts, d), lambda b, i, *_: (b * nt + i, 0))

    def const(shape):
        k = len(shape)
        return pl.BlockSpec(shape, lambda b, i, *_: (0,) * k, pipeline_mode=pl.Buffered(1))

    grid_spec = pltpu.PrefetchScalarGridSpec(
        num_scalar_prefetch=1,
        grid=(batch, nt),
        in_specs=[row, const((1, d)), const((1, d)), const((d, ab_in)), const((CONV_K, CONV_CH)),
                  const((1, CONV_CH)), const((1, CONV_CH)), const((1, CONV_CH)),
                  const(bias.shape), const((d, d))],
        out_specs=row,
        scratch_shapes=[pltpu.VMEM((CONV_CH // LANES, CONV_HALO + ts, LANES), F32),
                        pltpu.VMEM((BLK + ts, LANES), BF16),
                        pltpu.VMEM((BLK + ts, LANES), BF16),
                        pltpu.VMEM((CONV_CH // LANES, ts, LANES), F32),
                        pltpu.VMEM((ts, d), BF16)])
    return pl.pallas_call(
        _even_body,
        grid_spec=grid_spec,
        out_shape=jax.ShapeDtypeStruct((n, d), F32),
        compiler_params=pltpu.CompilerParams(
            dimension_semantics=("parallel", "arbitrary"),
            vmem_limit_bytes=48 * MIB),
        name="even_mixer",
    )(sinks, x, g2.reshape(1, d), g3.reshape(1, d), w_in, conv_w, conv_b.reshape(1, -1),
      ln_g.reshape(1, -1), ln_b.reshape(1, -1), bias, w_out)


def _fox_proj_body(x_ref, g_ref, wq_ref, wk_ref, wv_ref, wf_ref, bf_ref, tri_ref, eq_ref, ek_ref,
                   q_ref, k_ref, v_ref, carry_ref):
    i = pl.program_id(1)
    ts = x_ref.shape[0]

    @pl.when(i == 0)
    def _():
        carry_ref[...] = jnp.zeros(carry_ref.shape, F32)

    h = _rms(x_ref[...], g_ref[...]).astype(BF16)

    t = _dot(h, wf_ref[...]) + bf_ref[...]
    logf = jnp.minimum(t, 0.0) - jnp.log1p(jnp.exp(-jnp.abs(t)))
    tri = tri_ref[...]
    c = carry_ref[0:1, :]
    for piece in _split_bf16(logf):
        c = c + _dot(tri, piece)
    carry_ref[0:1, :] = c[ts - 1:ts, :]

    c1, c2, c3 = _split_bf16(c * LOG2E)
    lane = lax.broadcasted_iota(jnp.int32, (ts, LANES), 1)
    cp = jnp.where(lane < FOX_H, c1,
                   jnp.where(lane < 2 * FOX_H, c2,
                             jnp.where(lane < 3 * FOX_H, c3,
                                       jnp.where(lane < 4 * FOX_H, 1.0, 0.0).astype(BF16))))
    aq = _dot(cp, eq_ref[...])
    ak = _dot(cp, ek_ref[...])
    zq = _dot(h, wq_ref[...])
    zk = _dot(h, wk_ref[...])
    zv = _dot(h, wv_ref[...])
    low = lane < HEAD_DIM
    one_hi = (lane == HEAD_DIM).astype(F32)
    one_lo = (lane == 0).astype(F32)
    for p in range(FOX_H // 2):
        pair = slice(p * LANES, (p + 1) * LANES)
        even = slice(2 * p * LANES, (2 * p + 1) * LANES)
        odd = slice((2 * p + 1) * LANES, (2 * p + 2) * LANES)
        q_ref[:, even] = jnp.where(low, zq[:, pair], aq[:, pair]).astype(BF16)
        q_ref[:, odd] = jnp.where(low, aq[:, pair], zq[:, pair]).astype(BF16)
        k_ref[:, even] = jnp.where(low, zk[:, pair], ak[:, pair]).astype(BF16)
        k_ref[:, odd] = jnp.where(low, ak[:, pair], zk[:, pair]).astype(BF16)
        v_ref[:, even] = jnp.where(low, zv[:, pair], one_hi).astype(BF16)
        v_ref[:, odd] = jnp.where(low, one_lo, zv[:, pair]).astype(BF16)


def _fox_placement():
    eq = [[0.0] * (FOX_H // 2 * LANES) for _ in range(LANES)]
    ek = [[0.0] * (FOX_H // 2 * LANES) for _ in range(LANES)]
    for h in range(FOX_H):
        base = (h // 2) * LANES + (HEAD_DIM if h % 2 == 0 else 0)
        for a in range(N_AUG):
            eq[a * FOX_H + h][base + a] = 1.0
            eq[N_AUG * FOX_H + h][base + N_AUG + a] = 1.0
            ek[N_AUG * FOX_H + h][base + a] = 1.0
            ek[a * FOX_H + h][base + N_AUG + a] = -1.0
    return jnp.asarray(eq, BF16), jnp.asarray(ek, BF16)


def _fox_proj(x, batch, g, w_in, b_f):
    n, d = x.shape
    seq = n // batch
    ts = FOX_ROWS
    nt = seq // ts
    hd = FOX_H * HEAD_DIM
    wq = (w_in[:, :hd] * (LOG2E / math.sqrt(HEAD_DIM))).astype(BF16)
    wk = w_in[:, hd:2 * hd].astype(BF16)
    wv = w_in[:, 2 * hd:3 * hd].astype(BF16)
    reps = N_AUG
    wf = jnp.pad(jnp.tile(w_in[:, 3 * hd:], (1, reps)), ((0, 0), (0, LANES - reps * FOX_H))).astype(BF16)
    bf = jnp.pad(jnp.tile(b_f, reps), (0, LANES - reps * FOX_H)).reshape(1, LANES).astype(F32)
    tri = (jnp.arange(ts)[:, None] >= jnp.arange(ts)[None, :]).astype(BF16)
    eq, ek = _fox_placement()
    row = pl.BlockSpec((ts, d), lambda b, i: (b * nt + i, 0))
    wide = pl.BlockSpec((ts, FOX_H * LANES), lambda b, i: (b * nt + i, 0))
    return pl.pallas_call(
        _fox_proj_body,
        grid=(batch, nt),
        in_specs=[row, _const_spec((1, d)), _const_spec((d, hd)), _const_spec((d, hd)),
                  _const_spec((d, hd)), _const_spec((d, LANES)), _const_spec((1, LANES)),
                  _const_spec((ts, ts)), _const_spec(eq.shape), _const_spec(ek.shape)],
        out_specs=[wide, wide, wide],
        out_shape=[jax.ShapeDtypeStruct((n, FOX_H * LANES), BF16)] * 3,
        scratch_shapes=[pltpu.VMEM((8, LANES), F32)],
        compiler_params=pltpu.CompilerParams(
            dimension_semantics=("parallel", "arbitrary"),
            vmem_limit_bytes=48 * MIB),
        name="fox_proj",
    )(x, g.reshape(1, d), wq, wk, wv, wf, bf, tri, eq, ek)


def _fox_attn_body(q_ref, k_ref, v_ref, o_ref):
    seq = q_ref.shape[0]
    tq = FOX_TQ
    heads = [slice(hh * LANES, (hh + 1) * LANES) for hh in range(2)]

    def advance(state, rows, keys, diagonal):
        nr, nk = rows.stop - rows.start, keys.stop - keys.start
        new = []
        for (m, acc), head in zip(state, heads):
            s = _dot_nt(q_ref[rows, head], k_ref[keys, head])
            if diagonal:
                row = lax.broadcasted_iota(jnp.int32, (nr, nk), 0)
                col = lax.broadcasted_iota(jnp.int32, (nr, nk), 1)
                s = jnp.where(col - (nk - nr) <= row, s, NEG)
            m_new = jnp.maximum(m, jnp.max(s, axis=-1, keepdims=True))
            p = jnp.exp2(s - m_new).astype(BF16)
            acc = jnp.exp2(m - m_new) * acc + _dot(p, v_ref[keys, head])
            new.append((m_new, acc))
        return new

    def finish(state, rows):
        (_, acc0), (_, acc1) = state
        lane = lax.broadcasted_iota(jnp.int32, acc0.shape, 1)
        out0 = acc0 / acc0[:, HEAD_DIM:HEAD_DIM + 1]
        out1 = acc1 / acc1[:, 0:1]
        o_ref[rows, :] = jnp.where(lane < HEAD_DIM, out0, out1).astype(BF16)

    for qi in range(seq // tq):
        q0 = qi * tq
        rows = slice(q0, q0 + tq)
        state = [(jnp.full((tq, 1), NEG, F32), jnp.zeros((tq, LANES), F32))] * 2
        for j in range(qi + 1):
            state = advance(state, rows, slice(j * tq, (j + 1) * tq), j == qi)
        finish(state, rows)


def _fox_attn(q, k, v, batch):
    n = q.shape[0]
    seq = n // batch
    pairs = FOX_H // 2
    q = q.reshape(batch, seq, FOX_H * LANES)
    k = k.reshape(batch, seq, FOX_H * LANES)
    v = v.reshape(batch, seq, FOX_H * LANES)
    pair_block = pl.BlockSpec((None, seq, 2 * LANES), lambda b, p: (b, 0, p))
    out = pl.pallas_call(
        _fox_attn_body,
        grid=(batch, pairs),
        in_specs=[pair_block, pair_block, pair_block],
        out_specs=pl.BlockSpec((None, seq, LANES), lambda b, p: (b, 0, p)),
        out_shape=jax.ShapeDtypeStruct((batch, seq, FOX_H * HEAD_DIM), BF16),
        compiler_params=pltpu.CompilerParams(
            dimension_semantics=("parallel", "parallel"),
            vmem_limit_bytes=32 * MIB),
        name="fox_attn",
    )(q, k, v)
    return out.reshape(n, FOX_H * HEAD_DIM)


def kernel(x, norm_g, ffn_w_gate, ffn_w_up, ffn_w_down, ab_w_in, conv_w, conv_b, conv_ln_g,
           conv_ln_b, swa_sinks, ab_w_out, fox_w_in, fox_b_f, fox_w_out):
    batch, seq, d = x.shape
    depth = norm_g.shape[0]
    xf = x.reshape(batch * seq, d)
    wg, wu, wd = (w.astype(BF16) for w in (ffn_w_gate, ffn_w_up, ffn_w_down))
    for l in range(depth):
        g = norm_g[l]
        xf = _ffn(xf, g[0], g[1], wg, wu, wd, (l, 0))
        i = l // 2
        mix = None
        if l % 2 == 0:
            xf = _even_mixer(xf, batch, g[2], g[3], ab_w_in[i], conv_w[i], conv_b[i],
                             conv_ln_g[i], conv_ln_b[i], swa_sinks[i], ab_w_out[i])
        else:
            q, k, v = _fox_proj(xf, batch, g[2], fox_w_in[i], fox_b_f[i])
            mix = (_fox_attn(q, k, v, batch), fox_w_out[i], g[3])
        xf = _ffn(xf, g[4], g[5], wg, wu, wd, (l, 1), mix)
    return xf.reshape(batch, seq, d)
```

```python
import functools
import math

import jax
import jax.numpy as jnp
from jax import lax
from jax.experimental import pallas as pl
from jax.experimental.pallas import tpu as pltpu

F32 = jnp.float32
BF16 = jnp.bfloat16

LANES = 128
V7X_VMEM_BYTES = 64 * 1024 * 1024
MIB = 1024 * 1024

D_MODEL = 1024
D_FF = 2816
EPS = 1e-6
HEAD_DIM = 64
CONV_CH = 512
CONV_K = 31
SWA_HQ = 8
SWA_HKV = 2
SWA_G = SWA_HQ // SWA_HKV
WINDOW = 128
BLK = 128
FOX_H = 16
NEG = -1e30

FFN_ROWS = 1024
FFN_SUB_ROWS = 512
FFN_CHUNK = 256
MIX_ROWS = 512
CONV_HALO = 32
CONV_PHASES = 4
FOX_ROWS = 512
FOX_TQ = 512
FOX_TK = 256
N_AUG = 3
LOG2E = 1.4426950408889634


def _rms(x, g):
    ms = jnp.mean(x * x, axis=-1, keepdims=True)
    return x * lax.rsqrt(ms + EPS) * g


def _dot(a, b):
    return jnp.dot(a, b, preferred_element_type=F32)


def _dot_nt(a, b):
    return lax.dot_general(a, b, (((1,), (1,)), ((), ())), preferred_element_type=F32)


def _split_bf16(x):
    p1 = x.astype(BF16)
    r1 = x - p1.astype(F32)
    p2 = r1.astype(BF16)
    r2 = r1 - p2.astype(F32)
    return p1, p2, r2.astype(BF16)


def _const_spec(shape):
    n = len(shape)
    return pl.BlockSpec(shape, lambda *_: (0,) * n, pipeline_mode=pl.Buffered(1))


def _ffn_body(*refs, mix_in):
    if mix_in:
        attn_ref, wmix_ref, gmix_ref, *refs = refs
    x_ref, gpre_ref, gpost_ref, wg_ref, wu_ref, wd_ref, o_ref, a_ref = refs
    tf = FFN_CHUNK
    for r0 in range(0, x_ref.shape[0], FFN_SUB_ROWS):
        rows = slice(r0, r0 + FFN_SUB_ROWS)
        x = x_ref[rows, :]
        if mix_in:
            x = x + _rms(_dot(attn_ref[rows, :], wmix_ref[...]), gmix_ref[...])
        h = _rms(x, gpre_ref[...]).astype(BF16)
        for c in range(wg_ref.shape[1] // tf):
            cols = slice(c * tf, (c + 1) * tf)
            g = _dot(h, wg_ref[:, cols])
            u = _dot(h, wu_ref[:, cols])
            a_ref[rows, cols] = (g * jax.nn.sigmoid(g) * u).astype(BF16)
        y = _dot(a_ref[rows, :], wd_ref[...])
        o_ref[rows, :] = x + 0.5 * _rms(y, gpost_ref[...])


def _ffn(x, g_pre, g_post, wg, wu, wd, which, mix=None):
    n, d = x.shape
    tm = FFN_ROWS
    row = pl.BlockSpec((tm, d), lambda i: (i, 0))

    def stacked(rows, cols):
        return pl.BlockSpec((None, None, rows, cols), lambda i: which + (0, 0),
                            pipeline_mode=pl.Buffered(1))

    args = [x, g_pre.reshape(1, d), g_post.reshape(1, d), wg, wu, wd]
    in_specs = [row, _const_spec((1, d)), _const_spec((1, d)),
                stacked(d, D_FF), stacked(d, D_FF), stacked(D_FF, d)]
    weights = 3 * d * D_FF * 2
    tiles = 4 * tm * d * 4 + tm * D_FF * 2 + 4 * tm * FFN_CHUNK * 4 + 2 * tm * d * 4
    if mix is not None:
        attn, w_mix, g_mix = mix
        args = [attn, w_mix.astype(BF16), g_mix.reshape(1, d)] + args
        in_specs = [row, _const_spec((d, d)), _const_spec((1, d))] + in_specs
        weights += d * d * 2
        tiles += 2 * tm * d * 2 + tm * d * 4
    return pl.pallas_call(
        functools.partial(_ffn_body, mix_in=mix is not None),
        grid=(n // tm,),
        in_specs=in_specs,
        out_specs=row,
        out_shape=jax.ShapeDtypeStruct((n, d), F32),
        scratch_shapes=[pltpu.VMEM((tm, D_FF), BF16)],
        compiler_params=pltpu.CompilerParams(
            dimension_semantics=("parallel",),
            vmem_limit_bytes=min(V7X_VMEM_BYTES - 8 * MIB, weights + 2 * tiles)),
        name="ffn_mix" if mix is not None else "ffn",
    )(*args)


def _even_body(sinks_ref, x_ref, g2_ref, g3_ref, win_ref, cw_ref, cb_ref, lng_ref, lnb_ref,
               bias_ref, wout_ref, o_ref, a_buf, k_buf, v_buf, y_buf, cat_buf):
    i = pl.program_id(1)
    ts = x_ref.shape[0]
    x = x_ref[...]
    h = _rms(x, g2_ref[...]).astype(BF16)
    z = _dot(h, win_ref[...])
    o_gate = CONV_CH
    o_q = 2 * CONV_CH
    o_k = o_q + SWA_HQ * HEAD_DIM
    o_v = o_k + SWA_HKV * HEAD_DIM

    n_slabs = CONV_CH // LANES

    @pl.when(i == 0)
    def _():
        a_buf[:, 0:CONV_HALO, :] = jnp.zeros((n_slabs, CONV_HALO, LANES), F32)
        k_buf[0:BLK, :] = jnp.zeros((BLK, LANES), BF16)
        v_buf[0:BLK, :] = jnp.zeros((BLK, LANES), BF16)

    @pl.when(i > 0)
    def _():
        a_buf[:, 0:CONV_HALO, :] = a_buf[:, ts:ts + CONV_HALO, :]
        k_buf[0:BLK, :] = k_buf[ts:ts + BLK, :]
        v_buf[0:BLK, :] = v_buf[ts:ts + BLK, :]

    a = z[:, 0:o_gate] * jax.nn.sigmoid(z[:, o_gate:o_q])
    for cb in range(n_slabs):
        a_buf[cb, CONV_HALO:CONV_HALO + ts, :] = a[:, cb * LANES:(cb + 1) * LANES]
    k_buf[BLK:BLK + ts, :] = z[:, o_k:o_v].astype(BF16)
    v_buf[BLK:BLK + ts, :] = z[:, o_v:o_v + SWA_HKV * HEAD_DIM].astype(BF16)

    first_tap = CONV_HALO - (CONV_K - 1)
    n_rows = ts // CONV_PHASES
    for cb in range(n_slabs):
        cols = slice(cb * LANES, (cb + 1) * LANES)
        bias = jnp.broadcast_to(cb_ref[:, cols], (n_rows, LANES))
        for r in range(CONV_PHASES):
            acc = bias
            for j in range(CONV_K):
                taps = a_buf[cb, pl.ds(first_tap + r + j, n_rows, stride=CONV_PHASES), :]
                acc = acc + cw_ref[j:j + 1, cols] * taps
            y_buf[cb, pl.ds(r, n_rows, stride=CONV_PHASES), :] = acc
    ys = [y_buf[cb] for cb in range(n_slabs)]
    mu = sum(jnp.sum(y, axis=-1, keepdims=True) for y in ys) * (1.0 / CONV_CH)
    ycs = [y - mu for y in ys]
    var = sum(jnp.sum(yc * yc, axis=-1, keepdims=True) for yc in ycs) * (1.0 / CONV_CH)
    inv = lax.rsqrt(var + EPS)
    for cb in range(n_slabs):
        cols = slice(cb * LANES, (cb + 1) * LANES)
        yn = ycs[cb] * inv * lng_ref[:, cols] + lnb_ref[:, cols]
        cat_buf[:, cols] = (yn * jax.nn.sigmoid(yn)).astype(BF16)

    lane = lax.broadcasted_iota(jnp.int32, (BLK, LANES), 1)
    key_is_prev = lax.broadcasted_iota(jnp.int32, (SWA_G * BLK, 2 * BLK), 1) < BLK
    for qb in range(ts // BLK):
        rows = slice(qb * BLK, (qb + 1) * BLK)
        kk = k_buf[qb * BLK:qb * BLK + 2 * BLK, :]
        vv = v_buf[qb * BLK:qb * BLK + 2 * BLK, :]
        no_prev = jnp.logical_and(i == 0, qb == 0)
        outs = []
        for hk in range(SWA_HKV):
            mine = (lane < HEAD_DIM) if hk == 0 else (lane >= HEAD_DIM)
            qs = jnp.concatenate(
                [jnp.where(mine, z[rows, o_q + j * LANES:o_q + (j + 1) * LANES], 0.0)
                 for j in range(SWA_G)], axis=0).astype(BF16)
            s = _dot_nt(qs, kk)
            bias = jnp.where(jnp.logical_and(no_prev, key_is_prev), NEG, bias_ref[hk])
            s = s + bias
            sink = jnp.concatenate(
                [jnp.full((BLK, 1), sinks_ref[SWA_G * hk + j], F32) for j in range(SWA_G)], axis=0)
            m = jnp.maximum(jnp.max(s, axis=-1, keepdims=True), sink)
            p = jnp.exp(s - m)
            den = jnp.sum(p, axis=-1, keepdims=True) + jnp.exp(sink - m)
            outs.append(_dot(p.astype(BF16), vv) / den)
        for j in range(SWA_G):
            blk = jnp.where(lane < HEAD_DIM, outs[0][j * BLK:(j + 1) * BLK],
                            outs[1][j * BLK:(j + 1) * BLK])
            cat_buf[rows, CONV_CH + j * LANES:CONV_CH + (j + 1) * LANES] = blk.astype(BF16)

    y2 = _dot(cat_buf[...], wout_ref[...])
    o_ref[...] = x + _rms(y2, g3_ref[...])


def _swa_bias_table():
    t_loc = jnp.arange(BLK)[:, None]
    s_loc = jnp.arange(2 * BLK)[None, :]
    dist = t_loc + BLK - s_loc
    valid = (dist >= 0) & (dist < WINDOW)
    slopes = [2.0 ** (-8.0 * (g + 1) / SWA_HQ) for g in range(SWA_HQ)]
    per_head = [jnp.where(valid, -jnp.float32(sl) * dist.astype(F32), NEG) for sl in slopes]
    return jnp.stack([jnp.concatenate(per_head[SWA_G * hk:SWA_G * (hk + 1)], axis=0)
                      for hk in range(SWA_HKV)]).astype(F32)


def _pair_heads(w, axis):
    shape = w.shape
    w = jnp.moveaxis(w, axis, 0).reshape((SWA_HKV, SWA_G, HEAD_DIM) + shape[:axis] + shape[axis + 1:])
    w = jnp.swapaxes(w, 0, 1).reshape((SWA_HQ * HEAD_DIM,) + shape[:axis] + shape[axis + 1:])
    return jnp.moveaxis(w, 0, axis)


def _even_mixer(x, batch, g2, g3, w_in, conv_w, conv_b, ln_g, ln_b, sinks, w_out):
    n, d = x.shape
    seq = n // batch
    ts = MIX_ROWS
    nt = seq // ts
    o_q = 2 * CONV_CH
    o_k = o_q + SWA_HQ * HEAD_DIM
    ab_in = w_in.shape[1]
    wq = _pair_heads(w_in[:, o_q:o_k], 1) * (1.0 / math.sqrt(HEAD_DIM))
    w_in = jnp.concatenate([w_in[:, :o_q], wq, w_in[:, o_k:]], axis=1).astype(BF16)
    w_out = jnp.concatenate([w_out[:CONV_CH], _pair_heads(w_out[CONV_CH:], 0)], axis=0).astype(BF16)
    bias = _swa_bias_table()
    row = pl.BlockSpec((ts, d), lambda b, i, *_: (b * nt + i, 0))

    def const(shape):
        k = len(shape)
        return pl.BlockSpec(shape, lambda b, i, *_: (0,) * k, pipeline_mode=pl.Buffered(1))

    grid_spec = pltpu.PrefetchScalarGridSpec(
        num_scalar_prefetch=1,
        grid=(batch, nt),
        in_specs=[row, const((1, d)), const((1, d)), const((d, ab_in)), const((CONV_K, CONV_CH)),
                  const((1, CONV_CH)), const((1, CONV_CH)), const((1, CONV_CH)),
                  const(bias.shape), const((d, d))],
        out_specs=row,
        scratch_shapes=[pltpu.VMEM((CONV_CH // LANES, CONV_HALO + ts, LANES), F32),
                        pltpu.VMEM((BLK + ts, LANES), BF16),
                        pltpu.VMEM((BLK + ts, LANES), BF16),
                        pltpu.VMEM((CONV_CH // LANES, ts, LANES), F32),
                        pltpu.VMEM((ts, d), BF16)])
    return pl.pallas_call(
        _even_body,
        grid_spec=grid_spec,
        out_shape=jax.ShapeDtypeStruct((n, d), F32),
        compiler_params=pltpu.CompilerParams(
            dimension_semantics=("parallel", "arbitrary"),
            vmem_limit_bytes=48 * MIB),
        name="even_mixer",
    )(sinks, x, g2.reshape(1, d), g3.reshape(1, d), w_in, conv_w, conv_b.reshape(1, -1),
      ln_g.reshape(1, -1), ln_b.reshape(1, -1), bias, w_out)


def _fox_proj_body(x_ref, g_ref, wq_ref, wk_ref, wvt_ref, wf_ref, bf_ref, tri_ref, eq_ref, ek_ref,
                   q_ref, k_ref, vt_ref, carry_ref):
    i = pl.program_id(1)
    ts = x_ref.shape[0]

    @pl.when(i == 0)
    def _():
        carry_ref[...] = jnp.zeros(carry_ref.shape, F32)

    h = _rms(x_ref[...], g_ref[...]).astype(BF16)

    t = _dot(h, wf_ref[...]) + bf_ref[...]
    logf = jnp.minimum(t, 0.0) - jnp.log1p(jnp.exp(-jnp.abs(t)))
    tri = tri_ref[...]
    c = carry_ref[0:1, :]
    for piece in _split_bf16(logf):
        c = c + _dot(tri, piece)
    carry_ref[0:1, :] = c[ts - 1:ts, :]

    c1, c2, c3 = _split_bf16(c * LOG2E)
    lane = lax.broadcasted_iota(jnp.int32, (ts, LANES), 1)
    cp = jnp.where(lane < FOX_H, c1,
                   jnp.where(lane < 2 * FOX_H, c2,
                             jnp.where(lane < 3 * FOX_H, c3,
                                       jnp.where(lane < 4 * FOX_H, 1.0, 0.0).astype(BF16))))
    aq = _dot(cp, eq_ref[...])
    ak = _dot(cp, ek_ref[...])
    zq = _dot(h, wq_ref[...])
    zk = _dot(h, wk_ref[...])
    zvt = _dot_nt(wvt_ref[...], h)
    low = lane < HEAD_DIM
    row = lax.broadcasted_iota(jnp.int32, (LANES, ts), 0)
    top = row < HEAD_DIM
    one_mid = (row == HEAD_DIM).astype(F32)
    one_top = (row == 0).astype(F32)
    for p in range(FOX_H // 2):
        pair = slice(p * LANES, (p + 1) * LANES)
        even = slice(2 * p * LANES, (2 * p + 1) * LANES)
        odd = slice((2 * p + 1) * LANES, (2 * p + 2) * LANES)
        q_ref[:, even] = jnp.where(low, zq[:, pair], aq[:, pair]).astype(BF16)
        q_ref[:, odd] = jnp.where(low, aq[:, pair], zq[:, pair]).astype(BF16)
        k_ref[:, even] = jnp.where(low, zk[:, pair], ak[:, pair]).astype(BF16)
        k_ref[:, odd] = jnp.where(low, ak[:, pair], zk[:, pair]).astype(BF16)
        vt_ref[even, :] = jnp.where(top, zvt[pair, :], one_mid).astype(BF16)
        vt_ref[odd, :] = jnp.where(top, one_top, zvt[pair, :]).astype(BF16)


def _fox_placement():
    eq = [[0.0] * (FOX_H // 2 * LANES) for _ in range(LANES)]
    ek = [[0.0] * (FOX_H // 2 * LANES) for _ in range(LANES)]
    for h in range(FOX_H):
        base = (h // 2) * LANES + (HEAD_DIM if h % 2 == 0 else 0)
        for a in range(N_AUG):
            eq[a * FOX_H + h][base + a] = 1.0
            eq[N_AUG * FOX_H + h][base + N_AUG + a] = 1.0
            ek[N_AUG * FOX_H + h][base + a] = 1.0
            ek[a * FOX_H + h][base + N_AUG + a] = -1.0
    return jnp.asarray(eq, BF16), jnp.asarray(ek, BF16)


def _fox_proj(x, batch, g, w_in, b_f):
    n, d = x.shape
    seq = n // batch
    ts = FOX_ROWS
    nt = seq // ts
    hd = FOX_H * HEAD_DIM
    wq = (w_in[:, :hd] * (LOG2E / math.sqrt(HEAD_DIM))).astype(BF16)
    wk = w_in[:, hd:2 * hd].astype(BF16)
    wvt = w_in[:, 2 * hd:3 * hd].T.astype(BF16)
    reps = N_AUG
    wf = jnp.pad(jnp.tile(w_in[:, 3 * hd:], (1, reps)), ((0, 0), (0, LANES - reps * FOX_H))).astype(BF16)
    bf = jnp.pad(jnp.tile(b_f, reps), (0, LANES - reps * FOX_H)).reshape(1, LANES).astype(F32)
    tri = (jnp.arange(ts)[:, None] >= jnp.arange(ts)[None, :]).astype(BF16)
    eq, ek = _fox_placement()
    row = pl.BlockSpec((ts, d), lambda b, i: (b * nt + i, 0))
    wide = pl.BlockSpec((ts, FOX_H * LANES), lambda b, i: (b * nt + i, 0))
    tall = pl.BlockSpec((None, FOX_H * LANES, ts), lambda b, i: (b, 0, i))
    return pl.pallas_call(
        _fox_proj_body,
        grid=(batch, nt),
        in_specs=[row, _const_spec((1, d)), _const_spec((d, hd)), _const_spec((d, hd)),
                  _const_spec((hd, d)), _const_spec((d, LANES)), _const_spec((1, LANES)),
                  _const_spec((ts, ts)), _const_spec(eq.shape), _const_spec(ek.shape)],
        out_specs=[wide, wide, tall],
        out_shape=[jax.ShapeDtypeStruct((n, FOX_H * LANES), BF16),
                   jax.ShapeDtypeStruct((n, FOX_H * LANES), BF16),
                   jax.ShapeDtypeStruct((batch, FOX_H * LANES, seq), BF16)],
        scratch_shapes=[pltpu.VMEM((8, LANES), F32)],
        compiler_params=pltpu.CompilerParams(
            dimension_semantics=("parallel", "arbitrary"),
            vmem_limit_bytes=48 * MIB),
        name="fox_proj",
    )(x, g.reshape(1, d), wq, wk, wvt, wf, bf, tri, eq, ek)


def _fox_attn_body(q_ref, k_ref, vt_ref, o_ref):
    seq = q_ref.shape[0]
    tq = FOX_TQ
    tk = FOX_TK
    per_q = tq // tk
    row = lax.broadcasted_iota(jnp.int32, (LANES, tq), 0)
    key = lax.broadcasted_iota(jnp.int32, (tk, tq), 0)
    qry = lax.broadcasted_iota(jnp.int32, (tk, tq), 1)
    chains = [(qi, hh) for qi in reversed(range(seq // tq)) for hh in range(2)]

    def scores(chain, kb):
        qi, hh = chain
        head = slice(hh * LANES, (hh + 1) * LANES)
        st = _dot_nt(k_ref[kb * tk:(kb + 1) * tk, head], q_ref[qi * tq:(qi + 1) * tq, head])
        if kb >= qi * per_q:
            st = jnp.where(key + (kb * tk - qi * tq) <= qry, st, NEG)
        return st

    def weighted_values(chain, kb, st, m):
        _, hh = chain
        head = slice(hh * LANES, (hh + 1) * LANES)
        return _dot(vt_ref[head, kb * tk:(kb + 1) * tk], jnp.exp2(st - m).astype(BF16))

    done = {}
    prev = None
    for cur in chains + [None]:
        n_cur = (cur[0] + 1) * per_q if cur is not None else 0
        n_prev = (prev[0][0] + 1) * per_q if prev is not None else 0
        sts, acc = [], None
        for kb in range(max(n_cur, n_prev)):
            if kb < n_cur:
                sts.append(scores(cur, kb))
            if kb < n_prev:
                part = weighted_values(prev[0], kb, prev[1][kb], prev[2])
                acc = part if acc is None else acc + part
        if prev is not None:
            qi, hh = prev[0]
            ones_row = HEAD_DIM if hh == 0 else 0
            done[hh] = acc / acc[ones_row:ones_row + 1, :]
            if hh == 1:
                out = jnp.where(row < HEAD_DIM, done[0], done[1])
                o_ref[qi * tq:(qi + 1) * tq, :] = out.T.astype(BF16)
        if cur is not None:
            m = functools.reduce(jnp.maximum, [jnp.max(st, axis=0, keepdims=True) for st in sts])
            prev = (cur, sts, m)


def _fox_attn(q, k, vt, batch):
    n = q.shape[0]
    seq = n // batch
    pairs = FOX_H // 2
    q = q.reshape(batch, seq, FOX_H * LANES)
    k = k.reshape(batch, seq, FOX_H * LANES)
    pair_block = pl.BlockSpec((None, seq, 2 * LANES), lambda b, p: (b, 0, p))
    out = pl.pallas_call(
        _fox_attn_body,
        grid=(batch, pairs),
        in_specs=[pair_block, pair_block,
                  pl.BlockSpec((None, 2 * LANES, seq), lambda b, p: (b, p, 0))],
        out_specs=pl.BlockSpec((None, seq, LANES), lambda b, p: (b, 0, p)),
        out_shape=jax.ShapeDtypeStruct((batch, seq, FOX_H * HEAD_DIM), BF16),
        compiler_params=pltpu.CompilerParams(
            dimension_semantics=("parallel", "parallel"),
            vmem_limit_bytes=48 * MIB),
        name="fox_attn",
    )(q, k, vt)
    return out.reshape(n, FOX_H * HEAD_DIM)


def kernel(x, norm_g, ffn_w_gate, ffn_w_up, ffn_w_down, ab_w_in, conv_w, conv_b, conv_ln_g,
           conv_ln_b, swa_sinks, ab_w_out, fox_w_in, fox_b_f, fox_w_out):
    batch, seq, d = x.shape
    depth = norm_g.shape[0]
    xf = x.reshape(batch * seq, d)
    wg, wu, wd = (w.astype(BF16) for w in (ffn_w_gate, ffn_w_up, ffn_w_down))
    for l in range(depth):
        g = norm_g[l]
        xf = _ffn(xf, g[0], g[1], wg, wu, wd, (l, 0))
        i = l // 2
        mix = None
        if l % 2 == 0:
            xf = _even_mixer(xf, batch, g[2], g[3], ab_w_in[i], conv_w[i], conv_b[i],
                             conv_ln_g[i], conv_ln_b[i], swa_sinks[i], ab_w_out[i])
        else:
            q, k, vt = _fox_proj(xf, batch, g[2], fox_w_in[i], fox_b_f[i])
            mix = (_fox_attn(q, k, vt, batch), fox_w_out[i], g[3])
        xf = _ffn(xf, g[4], g[5], wg, wu, wd, (l, 1), mix)
    return xf.reshape(batch, seq, d)
```

```python
import functools
import math

import jax
import jax.numpy as jnp
from jax import lax
from jax.experimental import pallas as pl
from jax.experimental.pallas import tpu as pltpu

F32 = jnp.float32
BF16 = jnp.bfloat16

LANES = 128
V7X_VMEM_BYTES = 64 * 1024 * 1024
MIB = 1024 * 1024

D_MODEL = 1024
D_FF = 2816
EPS = 1e-6
HEAD_DIM = 64
CONV_CH = 512
CONV_K = 31
SWA_HQ = 8
SWA_HKV = 2
SWA_G = SWA_HQ // SWA_HKV
WINDOW = 128
BLK = 128
FOX_H = 16
NEG = -1e30

FFN_ROWS = 1024
FFN_SUB_ROWS = 512
FFN_CHUNK = 256
MIX_ROWS = 512
CONV_HALO = 32
CONV_PHASES = 4
FOX_ROWS = 512
FOX_TQ = 512
FOX_TK = 256
N_AUG = 3
LOG2E = 1.4426950408889634


def _rms(x, g):
    ms = jnp.mean(x * x, axis=-1, keepdims=True)
    return x * lax.rsqrt(ms + EPS) * g


def _dot(a, b):
    return jnp.dot(a, b, preferred_element_type=F32)


def _dot_nt(a, b):
    return lax.dot_general(a, b, (((1,), (1,)), ((), ())), preferred_element_type=F32)


def _split_bf16(x):
    p1 = x.astype(BF16)
    r1 = x - p1.astype(F32)
    p2 = r1.astype(BF16)
    r2 = r1 - p2.astype(F32)
    return p1, p2, r2.astype(BF16)


def _const_spec(shape):
    n = len(shape)
    return pl.BlockSpec(shape, lambda *_: (0,) * n, pipeline_mode=pl.Buffered(1))


def _ffn_body(*refs, mix_in):
    if mix_in:
        attn_ref, wmix_ref, gmix_ref, *refs = refs
    x_ref, gpre_ref, gpost_ref, wg_ref, wu_ref, wd_ref, o_ref, a_ref = refs
    tf = FFN_CHUNK
    for r0 in range(0, x_ref.shape[0], FFN_SUB_ROWS):
        rows = slice(r0, r0 + FFN_SUB_ROWS)
        x = x_ref[rows, :]
        if mix_in:
            x = x + _rms(_dot(attn_ref[rows, :], wmix_ref[...]), gmix_ref[...])
        h = _rms(x, gpre_ref[...]).astype(BF16)
        for c in range(wg_ref.shape[1] // tf):
            cols = slice(c * tf, (c + 1) * tf)
            g = _dot(h, wg_ref[:, cols])
            u = _dot(h, wu_ref[:, cols])
            a_ref[rows, cols] = (g * jax.nn.sigmoid(g) * u).astype(BF16)
        y = _dot(a_ref[rows, :], wd_ref[...])
        o_ref[rows, :] = x + 0.5 * _rms(y, gpost_ref[...])


def _ffn(x, g_pre, g_post, wg, wu, wd, which, mix=None):
    n, d = x.shape
    tm = FFN_ROWS
    row = pl.BlockSpec((tm, d), lambda i: (i, 0))

    def stacked(rows, cols):
        return pl.BlockSpec((None, None, rows, cols), lambda i: which + (0, 0),
                            pipeline_mode=pl.Buffered(1))

    args = [x, g_pre.reshape(1, d), g_post.reshape(1, d), wg, wu, wd]
    in_specs = [row, _const_spec((1, d)), _const_spec((1, d)),
                stacked(d, D_FF), stacked(d, D_FF), stacked(D_FF, d)]
    weights = 3 * d * D_FF * 2
    tiles = 4 * tm * d * 4 + tm * D_FF * 2 + 4 * tm * FFN_CHUNK * 4 + 2 * tm * d * 4
    if mix is not None:
        attn, w_mix, g_mix = mix
        args = [attn, w_mix.astype(BF16), g_mix.reshape(1, d)] + args
        in_specs = [row, _const_spec((d, d)), _const_spec((1, d))] + in_specs
        weights += d * d * 2
        tiles += 2 * tm * d * 2 + tm * d * 4
    return pl.pallas_call(
        functools.partial(_ffn_body, mix_in=mix is not None),
        grid=(n // tm,),
        in_specs=in_specs,
        out_specs=row,
        out_shape=jax.ShapeDtypeStruct((n, d), F32),
        scratch_shapes=[pltpu.VMEM((tm, D_FF), BF16)],
        compiler_params=pltpu.CompilerParams(
            dimension_semantics=("parallel",),
            vmem_limit_bytes=min(V7X_VMEM_BYTES - 8 * MIB, weights + 2 * tiles)),
        name="ffn_mix" if mix is not None else "ffn",
    )(*args)


def _even_body(sinks_ref, x_ref, g2_ref, g3_ref, win_ref, cw_ref, cb_ref, lng_ref, lnb_ref,
               bias_ref, wout_ref, o_ref, a_buf, k_buf, vt_buf, y_buf, cat_buf):
    i = pl.program_id(1)
    ts = x_ref.shape[0]
    x = x_ref[...]
    h = _rms(x, g2_ref[...]).astype(BF16)
    n_slabs = CONV_CH // LANES
    chunk = 2 * LANES

    @pl.when(i == 0)
    def _():
        a_buf[:, 0:CONV_HALO, :] = jnp.zeros((n_slabs, CONV_HALO, LANES), F32)
        k_buf[0:BLK, :] = jnp.zeros((BLK, LANES), BF16)
        vt_buf[:, 0:BLK] = jnp.zeros((LANES, BLK), BF16)

    @pl.when(i > 0)
    def _():
        a_buf[:, 0:CONV_HALO, :] = a_buf[:, ts:ts + CONV_HALO, :]
        k_buf[0:BLK, :] = k_buf[ts:ts + BLK, :]
        vt_buf[:, 0:BLK] = vt_buf[:, ts:ts + BLK]

    def project(c):
        return _dot(h, win_ref[:, c * chunk:(c + 1) * chunk])

    def gate_into_slab(cb, zc):
        a_buf[cb, CONV_HALO:CONV_HALO + ts, :] = zc[:, :LANES] * jax.nn.sigmoid(zc[:, LANES:])

    first_tap = CONV_HALO - (CONV_K - 1)
    n_rows = ts // CONV_PHASES

    def conv_phase(cb, r):
        cols = slice(cb * LANES, (cb + 1) * LANES)
        acc = jnp.broadcast_to(cb_ref[:, cols], (n_rows, LANES))
        for j in range(CONV_K):
            taps = a_buf[cb, pl.ds(first_tap + r + j, n_rows, stride=CONV_PHASES), :]
            acc = acc + cw_ref[j:j + 1, cols] * taps
        y_buf[cb, pl.ds(r, n_rows, stride=CONV_PHASES), :] = acc

    assert n_slabs == ts // BLK == 4
    gate_into_slab(0, project(0))
    zq = []
    for cb in range(n_slabs - 1):
        conv_phase(cb, 0)
        gate_into_slab(cb + 1, project(cb + 1))
        conv_phase(cb, 1)
        conv_phase(cb, 2)
        if cb < 2:
            zq.append(project(n_slabs + cb))
        else:
            zkv = project(n_slabs + 2)
            k_buf[BLK:BLK + ts, :] = zkv[:, :LANES].astype(BF16)
            vt_buf[:, BLK:BLK + ts] = zkv[:, LANES:].T.astype(BF16)
        conv_phase(cb, 3)

    lane = lax.broadcasted_iota(jnp.int32, (BLK, LANES), 1)
    key_is_prev = lax.broadcasted_iota(jnp.int32, (2 * BLK, SWA_G * BLK), 0) < BLK
    units = [(qb, hk) for qb in range(ts // BLK) for hk in range(SWA_HKV)]
    last = n_slabs - 1

    def unit_scores(qb, hk):
        rows = slice(qb * BLK, (qb + 1) * BLK)
        mine = (lane < HEAD_DIM) if hk == 0 else (lane >= HEAD_DIM)
        qs = jnp.concatenate(
            [jnp.where(mine, zq[j // 2][rows, (j % 2) * LANES:(j % 2 + 1) * LANES], 0.0)
             for j in range(SWA_G)], axis=0).astype(BF16)
        st = _dot_nt(k_buf[qb * BLK:qb * BLK + 2 * BLK, :], qs)
        bias = bias_ref[hk]
        if qb == 0:
            bias = jnp.where(jnp.logical_and(i == 0, key_is_prev), NEG, bias)
        return st + bias

    def unit_softmax(hk, st):
        sink = jnp.concatenate(
            [jnp.full((1, BLK), sinks_ref[SWA_G * hk + j], F32) for j in range(SWA_G)], axis=1)
        m = jnp.maximum(jnp.max(st, axis=0, keepdims=True), sink)
        p = jnp.exp2(st - m)
        den = jnp.sum(p, axis=0, keepdims=True) + jnp.exp2(sink - m)
        return p.astype(BF16), den

    def unit_values(qb, p, den):
        return _dot(vt_buf[:, qb * BLK:qb * BLK + 2 * BLK], p) / den

    conv_phase(last, 0)
    sts = [unit_scores(qb, hk) for qb, hk in units]
    conv_phase(last, 1)
    pds = [unit_softmax(hk, st) for (qb, hk), st in zip(units, sts)]
    conv_phase(last, 2)
    ots = [unit_values(qb, p, den) for (qb, hk), (p, den) in zip(units, pds)]
    conv_phase(last, 3)
    row = lax.broadcasted_iota(jnp.int32, (LANES, SWA_G * BLK), 0)
    for qb in range(ts // BLK):
        out = jnp.where(row < HEAD_DIM, ots[SWA_HKV * qb], ots[SWA_HKV * qb + 1]).T
        for j in range(SWA_G):
            cat_buf[qb * BLK:(qb + 1) * BLK, CONV_CH + j * LANES:CONV_CH + (j + 1) * LANES] = (
                out[j * BLK:(j + 1) * BLK].astype(BF16))

    ys = [y_buf[cb] for cb in range(n_slabs)]
    mu = jnp.sum(sum(ys), axis=-1, keepdims=True) * (1.0 / CONV_CH)
    ycs = [y - mu for y in ys]
    var = jnp.sum(sum(yc * yc for yc in ycs), axis=-1, keepdims=True) * (1.0 / CONV_CH)
    inv = lax.rsqrt(var + EPS)
    for cb in range(n_slabs):
        cols = slice(cb * LANES, (cb + 1) * LANES)
        yn = ycs[cb] * inv * lng_ref[:, cols] + lnb_ref[:, cols]
        cat_buf[:, cols] = (yn * jax.nn.sigmoid(yn)).astype(BF16)

    y2 = _dot(cat_buf[...], wout_ref[...])
    o_ref[...] = x + _rms(y2, g3_ref[...])


def _swa_bias_table():
    t_loc = jnp.arange(BLK)[None, :]
    s_loc = jnp.arange(2 * BLK)[:, None]
    dist = t_loc + BLK - s_loc
    valid = (dist >= 0) & (dist < WINDOW)
    slopes = [2.0 ** (-8.0 * (g + 1) / SWA_HQ) for g in range(SWA_HQ)]
    per_head = [jnp.where(valid, -jnp.float32(sl * LOG2E) * dist.astype(F32), NEG) for sl in slopes]
    return jnp.stack([jnp.concatenate(per_head[SWA_G * hk:SWA_G * (hk + 1)], axis=1)
                      for hk in range(SWA_HKV)]).astype(F32)


def _pair_heads(w, axis):
    shape = w.shape
    w = jnp.moveaxis(w, axis, 0).reshape((SWA_HKV, SWA_G, HEAD_DIM) + shape[:axis] + shape[axis + 1:])
    w = jnp.swapaxes(w, 0, 1).reshape((SWA_HQ * HEAD_DIM,) + shape[:axis] + shape[axis + 1:])
    return jnp.moveaxis(w, 0, axis)


def _even_mixer(x, batch, g2, g3, w_in, conv_w, conv_b, ln_g, ln_b, sinks, w_out):
    n, d = x.shape
    seq = n // batch
    ts = MIX_ROWS
    nt = seq // ts
    o_q = 2 * CONV_CH
    o_k = o_q + SWA_HQ * HEAD_DIM
    ab_in = w_in.shape[1]
    wq = _pair_heads(w_in[:, o_q:o_k], 1) * (LOG2E / math.sqrt(HEAD_DIM))
    slabs = [w_in[:, half + cb * LANES:half + (cb + 1) * LANES]
             for cb in range(CONV_CH // LANES) for half in (0, CONV_CH)]
    w_in = jnp.concatenate(slabs + [wq, w_in[:, o_k:]], axis=1).astype(BF16)
    w_out = jnp.concatenate([w_out[:CONV_CH], _pair_heads(w_out[CONV_CH:], 0)], axis=0).astype(BF16)
    bias = _swa_bias_table()
    row = pl.BlockSpec((ts, d), lambda b, i, *_: (b * nt + i, 0))

    def const(shape):
        k = len(shape)
        return pl.BlockSpec(shape, lambda b, i, *_: (0,) * k, pipeline_mode=pl.Buffered(1))

    grid_spec = pltpu.PrefetchScalarGridSpec(
        num_scalar_prefetch=1,
        grid=(batch, nt),
        in_specs=[row, const((1, d)), const((1, d)), const((d, ab_in)), const((CONV_K, CONV_CH)),
                  const((1, CONV_CH)), const((1, CONV_CH)), const((1, CONV_CH)),
                  const(bias.shape), const((d, d))],
        out_specs=row,
        scratch_shapes=[pltpu.VMEM((CONV_CH // LANES, CONV_HALO + ts, LANES), F32),
                        pltpu.VMEM((BLK + ts, LANES), BF16),
                        pltpu.VMEM((LANES, BLK + ts), BF16),
                        pltpu.VMEM((CONV_CH // LANES, ts, LANES), F32),
                        pltpu.VMEM((ts, d), BF16)])
    return pl.pallas_call(
        _even_body,
        grid_spec=grid_spec,
        out_shape=jax.ShapeDtypeStruct((n, d), F32),
        compiler_params=pltpu.CompilerParams(
            dimension_semantics=("parallel", "arbitrary"),
            vmem_limit_bytes=48 * MIB),
        name="even_mixer",
    )(sinks * LOG2E, x, g2.reshape(1, d), g3.reshape(1, d), w_in, conv_w, conv_b.reshape(1, -1),
      ln_g.reshape(1, -1), ln_b.reshape(1, -1), bias, w_out)


def _fox_proj_body(x_ref, g_ref, wq_ref, wk_ref, wvt_ref, wf_ref, bf_ref, tri_ref, eq_ref, ek_ref,
                   q_ref, k_ref, vt_ref, carry_ref):
    i = pl.program_id(1)
    ts = x_ref.shape[0]

    @pl.when(i == 0)
    def _():
        carry_ref[...] = jnp.zeros(carry_ref.shape, F32)

    h = _rms(x_ref[...], g_ref[...]).astype(BF16)

    t = _dot(h, wf_ref[...]) + bf_ref[...]
    zq = _dot(h, wq_ref[...])
    logf = jnp.minimum(t, 0.0) - jnp.log1p(jnp.exp(-jnp.abs(t)))
    tri = tri_ref[...]
    c = carry_ref[0:1, :]
    for piece in _split_bf16(logf):
        c = c + _dot(tri, piece)
    carry_ref[0:1, :] = c[ts - 1:ts, :]
    zk = _dot(h, wk_ref[...])

    c1, c2, c3 = _split_bf16(c * LOG2E)
    lane = lax.broadcasted_iota(jnp.int32, (ts, LANES), 1)
    cp = jnp.where(lane < FOX_H, c1,
                   jnp.where(lane < 2 * FOX_H, c2,
                             jnp.where(lane < 3 * FOX_H, c3,
                                       jnp.where(lane < 4 * FOX_H, 1.0, 0.0).astype(BF16))))
    aq = _dot(cp, eq_ref[...])
    ak = _dot(cp, ek_ref[...])
    zvt = _dot_nt(wvt_ref[...], h)
    low = lane < HEAD_DIM
    for p in range(FOX_H // 2):
        pair = slice(p * LANES, (p + 1) * LANES)
        even = slice(2 * p * LANES, (2 * p + 1) * LANES)
        odd = slice((2 * p + 1) * LANES, (2 * p + 2) * LANES)
        q_ref[:, even] = jnp.where(low, zq[:, pair], aq[:, pair]).astype(BF16)
        q_ref[:, odd] = jnp.where(low, aq[:, pair], zq[:, pair]).astype(BF16)
        k_ref[:, even] = jnp.where(low, zk[:, pair], ak[:, pair]).astype(BF16)
        k_ref[:, odd] = jnp.where(low, ak[:, pair], zk[:, pair]).astype(BF16)
    row = lax.broadcasted_iota(jnp.int32, (LANES, ts), 0)
    top = row < HEAD_DIM
    one_mid = (row == HEAD_DIM).astype(F32)
    one_top = (row == 0).astype(F32)
    for p in range(FOX_H // 2):
        pair = slice(p * LANES, (p + 1) * LANES)
        even = slice(2 * p * LANES, (2 * p + 1) * LANES)
        odd = slice((2 * p + 1) * LANES, (2 * p + 2) * LANES)
        vt_ref[even, :] = jnp.where(top, zvt[pair, :], one_mid).astype(BF16)
        vt_ref[odd, :] = jnp.where(top, one_top, zvt[pair, :]).astype(BF16)


def _fox_placement():
    eq = [[0.0] * (FOX_H // 2 * LANES) for _ in range(LANES)]
    ek = [[0.0] * (FOX_H // 2 * LANES) for _ in range(LANES)]
    for h in range(FOX_H):
        base = (h // 2) * LANES + (HEAD_DIM if h % 2 == 0 else 0)
        for a in range(N_AUG):
            eq[a * FOX_H + h][base + a] = 1.0
            eq[N_AUG * FOX_H + h][base + N_AUG + a] = 1.0
            ek[N_AUG * FOX_H + h][base + a] = 1.0
            ek[a * FOX_H + h][base + N_AUG + a] = -1.0
    return jnp.asarray(eq, BF16), jnp.asarray(ek, BF16)


def _fox_proj(x, batch, g, w_in, b_f):
    n, d = x.shape
    seq = n // batch
    ts = FOX_ROWS
    nt = seq // ts
    hd = FOX_H * HEAD_DIM
    wq = (w_in[:, :hd] * (LOG2E / math.sqrt(HEAD_DIM))).astype(BF16)
    wk = w_in[:, hd:2 * hd].astype(BF16)
    wvt = w_in[:, 2 * hd:3 * hd].T.astype(BF16)
    reps = N_AUG
    wf = jnp.pad(jnp.tile(w_in[:, 3 * hd:], (1, reps)), ((0, 0), (0, LANES - reps * FOX_H))).astype(BF16)
    bf = jnp.pad(jnp.tile(b_f, reps), (0, LANES - reps * FOX_H)).reshape(1, LANES).astype(F32)
    tri = (jnp.arange(ts)[:, None] >= jnp.arange(ts)[None, :]).astype(BF16)
    eq, ek = _fox_placement()
    row = pl.BlockSpec((ts, d), lambda b, i: (b * nt + i, 0))
    wide = pl.BlockSpec((ts, FOX_H * LANES), lambda b, i: (b * nt + i, 0))
    tall = pl.BlockSpec((None, FOX_H * LANES, ts), lambda b, i: (b, 0, i))
    return pl.pallas_call(
        _fox_proj_body,
        grid=(batch, nt),
        in_specs=[row, _const_spec((1, d)), _const_spec((d, hd)), _const_spec((d, hd)),
                  _const_spec((hd, d)), _const_spec((d, LANES)), _const_spec((1, LANES)),
                  _const_spec((ts, ts)), _const_spec(eq.shape), _const_spec(ek.shape)],
        out_specs=[wide, wide, tall],
        out_shape=[jax.ShapeDtypeStruct((n, FOX_H * LANES), BF16),
                   jax.ShapeDtypeStruct((n, FOX_H * LANES), BF16),
                   jax.ShapeDtypeStruct((batch, FOX_H * LANES, seq), BF16)],
        scratch_shapes=[pltpu.VMEM((8, LANES), F32)],
        compiler_params=pltpu.CompilerParams(
            dimension_semantics=("parallel", "arbitrary"),
            vmem_limit_bytes=48 * MIB),
        name="fox_proj",
    )(x, g.reshape(1, d), wq, wk, wvt, wf, bf, tri, eq, ek)


def _fox_attn_body(q_ref, k_ref, vt_ref, o_ref):
    seq = q_ref.shape[0]
    tq = FOX_TQ
    tk = FOX_TK
    per_q = tq // tk
    row = lax.broadcasted_iota(jnp.int32, (LANES, tq), 0)
    key = lax.broadcasted_iota(jnp.int32, (tk, tq), 0)
    qry = lax.broadcasted_iota(jnp.int32, (tk, tq), 1)
    chains = [(qi, hh) for qi in reversed(range(seq // tq)) for hh in range(2)]

    def scores(chain, kb):
        qi, hh = chain
        head = slice(hh * LANES, (hh + 1) * LANES)
        st = _dot_nt(k_ref[kb * tk:(kb + 1) * tk, head], q_ref[qi * tq:(qi + 1) * tq, head])
        if kb >= qi * per_q:
            st = jnp.where(key + (kb * tk - qi * tq) <= qry, st, NEG)
        return st

    def weighted_values(chain, kb, st, m):
        _, hh = chain
        head = slice(hh * LANES, (hh + 1) * LANES)
        return _dot(vt_ref[head, kb * tk:(kb + 1) * tk], jnp.exp2(st - m).astype(BF16))

    done = {}
    prev = None
    for cur in chains + [None]:
        n_cur = (cur[0] + 1) * per_q if cur is not None else 0
        n_prev = (prev[0][0] + 1) * per_q if prev is not None else 0
        sts, acc = [], None
        for kb in range(max(n_cur, n_prev)):
            if kb < n_cur:
                sts.append(scores(cur, kb))
            if kb < n_prev:
                part = weighted_values(prev[0], kb, prev[1][kb], prev[2])
                acc = part if acc is None else acc + part
        if prev is not None:
            qi, hh = prev[0]
            ones_row = HEAD_DIM if hh == 0 else 0
            done[hh] = acc / acc[ones_row:ones_row + 1, :]
            if hh == 1:
                out = jnp.where(row < HEAD_DIM, done[0], done[1])
                o_ref[qi * tq:(qi + 1) * tq, :] = out.T.astype(BF16)
        if cur is not None:
            m = functools.reduce(jnp.maximum, [jnp.max(st, axis=0, keepdims=True) for st in sts])
            prev = (cur, sts, m)


def _fox_attn(q, k, vt, batch):
    n = q.shape[0]
    seq = n // batch
    pairs = FOX_H // 2
    q = q.reshape(batch, seq, FOX_H * LANES)
    k = k.reshape(batch, seq, FOX_H * LANES)
    pair_block = pl.BlockSpec((None, seq, 2 * LANES), lambda b, p: (b, 0, p))
    out = pl.pallas_call(
        _fox_attn_body,
        grid=(batch, pairs),
        in_specs=[pair_block, pair_block,
                  pl.BlockSpec((None, 2 * LANES, seq), lambda b, p: (b, p, 0))],
        out_specs=pl.BlockSpec((None, seq, LANES), lambda b, p: (b, 0, p)),
        out_shape=jax.ShapeDtypeStruct((batch, seq, FOX_H * HEAD_DIM), BF16),
        compiler_params=pltpu.CompilerParams(
            dimension_semantics=("parallel", "parallel"),
            vmem_limit_bytes=48 * MIB),
        name="fox_attn",
    )(q, k, vt)
    return out.reshape(n, FOX_H * HEAD_DIM)


def kernel(x, norm_g, ffn_w_gate, ffn_w_up, ffn_w_down, ab_w_in, conv_w, conv_b, conv_ln_g,
           conv_ln_b, swa_sinks, ab_w_out, fox_w_in, fox_b_f, fox_w_out):
    batch, seq, d = x.shape
    depth = norm_g.shape[0]
    xf = x.reshape(batch * seq, d)
    wg, wu, wd = (w.astype(BF16) for w in (ffn_w_gate, ffn_w_up, ffn_w_down))
    for l in range(depth):
        g = norm_g[l]
        xf = _ffn(xf, g[0], g[1], wg, wu, wd, (l, 0))
        i = l // 2
        mix = None
        if l % 2 == 0:
            xf = _even_mixer(xf, batch, g[2], g[3], ab_w_in[i], conv_w[i], conv_b[i],
                             conv_ln_g[i], conv_ln_b[i], swa_sinks[i], ab_w_out[i])
        else:
            q, k, vt = _fox_proj(xf, batch, g[2], fox_w_in[i], fox_b_f[i])
            mix = (_fox_attn(q, k, vt, batch), fox_w_out[i], g[3])
        xf = _ffn(xf, g[4], g[5], wg, wu, wd, (l, 1), mix)
    return xf.reshape(batch, seq, d)
```

```python
import functools
import math

import jax
import jax.numpy as jnp
import numpy as np
from jax import lax
from jax.experimental import pallas as pl
from jax.experimental.pallas import tpu as pltpu

F32 = jnp.float32
BF16 = jnp.bfloat16

LANES = 128
V7X_VMEM_BYTES = 64 * 1024 * 1024
MIB = 1024 * 1024

D_MODEL = 1024
D_FF = 2816
EPS = 1e-6
HEAD_DIM = 64
CONV_CH = 512
CONV_K = 31
SWA_HQ = 8
SWA_HKV = 2
SWA_G = SWA_HQ // SWA_HKV
WINDOW = 128
BLK = 128
FOX_H = 16
NEG = -1e30

FFN_ROWS = 1024
FFN_SUB_ROWS = 512
FFN_CHUNK = 256
MIX_ROWS = 512
CONV_HALO = 32
CONV_PHASES = 4
FOX_ROWS = 512
FOX_TQ = 512
FOX_TK = 256
N_AUG = 3
LOG2E = 1.4426950408889634


def _rms(x, g):
    ms = jnp.mean(x * x, axis=-1, keepdims=True)
    return x * lax.rsqrt(ms + EPS) * g


def _dot(a, b):
    return jnp.dot(a, b, preferred_element_type=F32)


def _dot_nt(a, b):
    return lax.dot_general(a, b, (((1,), (1,)), ((), ())), preferred_element_type=F32)


def _split_bf16(x):
    p1 = x.astype(BF16)
    r1 = x - p1.astype(F32)
    p2 = r1.astype(BF16)
    r2 = r1 - p2.astype(F32)
    return p1, p2, r2.astype(BF16)


def _const_spec(shape):
    n = len(shape)
    return pl.BlockSpec(shape, lambda *_: (0,) * n, pipeline_mode=pl.Buffered(1))


def _pick_spec(index, shape):
    n = len(shape)
    return pl.BlockSpec((None,) + tuple(shape), lambda *_: (index,) + (0,) * n,
                        pipeline_mode=pl.Buffered(1))


def _ffn_body(*refs, mix_in):
    if mix_in:
        attn_ref, wmix_ref, gmix_ref, *refs = refs
    x_ref, gpre_ref, gpost_ref, wg_ref, wu_ref, wd_ref, o_ref, a_ref = refs
    tf = FFN_CHUNK
    for r0 in range(0, x_ref.shape[0], FFN_SUB_ROWS):
        rows = slice(r0, r0 + FFN_SUB_ROWS)
        x = x_ref[rows, :]
        if mix_in:
            x = x + _rms(_dot(attn_ref[rows, :], wmix_ref[...]), gmix_ref[...])
        h = _rms(x, gpre_ref[...]).astype(BF16)
        for c in range(wg_ref.shape[1] // tf):
            cols = slice(c * tf, (c + 1) * tf)
            g = _dot(h, wg_ref[:, cols])
            u = _dot(h, wu_ref[:, cols])
            a_ref[rows, cols] = (g * jax.nn.sigmoid(g) * u).astype(BF16)
        y = _dot(a_ref[rows, :], wd_ref[...])
        o_ref[rows, :] = x + 0.5 * _rms(y, gpost_ref[...])


def _ffn(x, gains, pre, post, wg, wu, wd, which, mix=None):
    n, d = x.shape
    tm = FFN_ROWS
    row = pl.BlockSpec((tm, d), lambda i: (i, 0))

    def stacked(rows, cols):
        return pl.BlockSpec((None, None, rows, cols), lambda i: which + (0, 0),
                            pipeline_mode=pl.Buffered(1))

    args = [x, gains, gains, wg, wu, wd]
    in_specs = [row, _pick_spec(pre, (1, d)), _pick_spec(post, (1, d)),
                stacked(d, D_FF), stacked(d, D_FF), stacked(D_FF, d)]
    weights = 3 * d * D_FF * 2
    tiles = 4 * tm * d * 4 + tm * D_FF * 2 + 4 * tm * FFN_CHUNK * 4 + 2 * tm * d * 4
    if mix is not None:
        attn, w_mix, g_mix = mix
        args = [attn, w_mix.astype(BF16), gains] + args
        in_specs = [row, _const_spec((d, d)), _pick_spec(g_mix, (1, d))] + in_specs
        weights += d * d * 2
        tiles += 2 * tm * d * 2 + tm * d * 4
    return pl.pallas_call(
        functools.partial(_ffn_body, mix_in=mix is not None),
        grid=(n // tm,),
        in_specs=in_specs,
        out_specs=row,
        out_shape=jax.ShapeDtypeStruct((n, d), F32),
        scratch_shapes=[pltpu.VMEM((tm, D_FF), BF16)],
        compiler_params=pltpu.CompilerParams(
            dimension_semantics=("parallel",),
            vmem_limit_bytes=min(V7X_VMEM_BYTES - 8 * MIB, weights + 2 * tiles)),
        name="ffn_mix" if mix is not None else "ffn",
    )(*args)


def _even_body(sinks_ref, x_ref, g2_ref, g3_ref, win_ref, cw_ref, cb_ref, lng_ref, lnb_ref,
               bias_ref, wout_ref, o_ref, a_buf, k_buf, vt_buf, y_buf, cat_buf):
    i = pl.program_id(1)
    ts = x_ref.shape[0]
    x = x_ref[...]
    h = _rms(x, g2_ref[...]).astype(BF16)
    n_slabs = CONV_CH // LANES
    chunk = 2 * LANES

    @pl.when(i == 0)
    def _():
        a_buf[:, 0:CONV_HALO, :] = jnp.zeros((n_slabs, CONV_HALO, LANES), F32)
        k_buf[0:BLK, :] = jnp.zeros((BLK, LANES), BF16)
        vt_buf[:, 0:BLK] = jnp.zeros((LANES, BLK), BF16)

    @pl.when(i > 0)
    def _():
        a_buf[:, 0:CONV_HALO, :] = a_buf[:, ts:ts + CONV_HALO, :]
        k_buf[0:BLK, :] = k_buf[ts:ts + BLK, :]
        vt_buf[:, 0:BLK] = vt_buf[:, ts:ts + BLK]

    def project(c):
        return _dot(h, win_ref[:, c * chunk:(c + 1) * chunk])

    def gate_into_slab(cb, zc):
        a_buf[cb, CONV_HALO:CONV_HALO + ts, :] = zc[:, :LANES] * jax.nn.sigmoid(zc[:, LANES:])

    first_tap = CONV_HALO - (CONV_K - 1)
    n_rows = ts // CONV_PHASES

    def conv_phase(cb, r):
        cols = slice(cb * LANES, (cb + 1) * LANES)
        acc = jnp.broadcast_to(cb_ref[:, cols], (n_rows, LANES))
        for j in range(CONV_K):
            taps = a_buf[cb, pl.ds(first_tap + r + j, n_rows, stride=CONV_PHASES), :]
            acc = acc + cw_ref[j:j + 1, cols] * taps
        y_buf[cb, pl.ds(r, n_rows, stride=CONV_PHASES), :] = acc

    assert n_slabs == ts // BLK == 4
    gate_into_slab(0, project(0))
    zq = []
    for cb in range(n_slabs - 1):
        conv_phase(cb, 0)
        gate_into_slab(cb + 1, project(cb + 1))
        conv_phase(cb, 1)
        conv_phase(cb, 2)
        if cb < 2:
            zq.append(project(n_slabs + cb))
        else:
            zkv = project(n_slabs + 2)
            k_buf[BLK:BLK + ts, :] = zkv[:, :LANES].astype(BF16)
            vt_buf[:, BLK:BLK + ts] = zkv[:, LANES:].T.astype(BF16)
        conv_phase(cb, 3)

    lane = lax.broadcasted_iota(jnp.int32, (BLK, LANES), 1)
    key_is_prev = lax.broadcasted_iota(jnp.int32, (2 * BLK, SWA_G * BLK), 0) < BLK
    units = [(qb, hk) for qb in range(ts // BLK) for hk in range(SWA_HKV)]
    last = n_slabs - 1

    def unit_scores(qb, hk):
        rows = slice(qb * BLK, (qb + 1) * BLK)
        mine = (lane < HEAD_DIM) if hk == 0 else (lane >= HEAD_DIM)
        qs = jnp.concatenate(
            [jnp.where(mine, zq[j // 2][rows, (j % 2) * LANES:(j % 2 + 1) * LANES], 0.0)
             for j in range(SWA_G)], axis=0).astype(BF16)
        st = _dot_nt(k_buf[qb * BLK:qb * BLK + 2 * BLK, :], qs)
        bias = bias_ref[hk]
        if qb == 0:
            bias = jnp.where(jnp.logical_and(i == 0, key_is_prev), NEG, bias)
        return st + bias

    def unit_softmax(hk, st):
        sink = jnp.concatenate(
            [jnp.full((1, BLK), sinks_ref[SWA_G * hk + j], F32) for j in range(SWA_G)], axis=1)
        m = jnp.maximum(jnp.max(st, axis=0, keepdims=True), sink)
        p = jnp.exp2(st - m)
        den = jnp.sum(p, axis=0, keepdims=True) + jnp.exp2(sink - m)
        return p.astype(BF16), den

    def unit_values(qb, p, den):
        return _dot(vt_buf[:, qb * BLK:qb * BLK + 2 * BLK], p) / den

    conv_phase(last, 0)
    sts = [unit_scores(qb, hk) for qb, hk in units]
    conv_phase(last, 1)
    pds = [unit_softmax(hk, st) for (qb, hk), st in zip(units, sts)]
    conv_phase(last, 2)
    ots = [unit_values(qb, p, den) for (qb, hk), (p, den) in zip(units, pds)]
    conv_phase(last, 3)
    row = lax.broadcasted_iota(jnp.int32, (LANES, SWA_G * BLK), 0)
    for qb in range(ts // BLK):
        out = jnp.where(row < HEAD_DIM, ots[SWA_HKV * qb], ots[SWA_HKV * qb + 1]).T
        for j in range(SWA_G):
            cat_buf[qb * BLK:(qb + 1) * BLK, CONV_CH + j * LANES:CONV_CH + (j + 1) * LANES] = (
                out[j * BLK:(j + 1) * BLK].astype(BF16))

    ys = [y_buf[cb] for cb in range(n_slabs)]
    mu = jnp.sum(sum(ys), axis=-1, keepdims=True) * (1.0 / CONV_CH)
    ycs = [y - mu for y in ys]
    var = jnp.sum(sum(yc * yc for yc in ycs), axis=-1, keepdims=True) * (1.0 / CONV_CH)
    inv = lax.rsqrt(var + EPS)
    for cb in range(n_slabs):
        cols = slice(cb * LANES, (cb + 1) * LANES)
        yn = ycs[cb] * inv * lng_ref[:, cols] + lnb_ref[:, cols]
        cat_buf[:, cols] = (yn * jax.nn.sigmoid(yn)).astype(BF16)

    y2 = _dot(cat_buf[...], wout_ref[...])
    o_ref[...] = x + _rms(y2, g3_ref[...])


def _swa_bias_table():
    t_loc = np.arange(BLK)[None, :]
    s_loc = np.arange(2 * BLK)[:, None]
    dist = t_loc + BLK - s_loc
    valid = (dist >= 0) & (dist < WINDOW)
    slopes = [2.0 ** (-8.0 * (g + 1) / SWA_HQ) for g in range(SWA_HQ)]
    per_head = [np.where(valid, -np.float32(sl * LOG2E) * dist.astype(np.float32), np.float32(NEG))
                for sl in slopes]
    return np.stack([np.concatenate(per_head[SWA_G * hk:SWA_G * (hk + 1)], axis=1)
                     for hk in range(SWA_HKV)]).astype(np.float32)


def _pair_heads(w, axis):
    shape = w.shape
    w = jnp.moveaxis(w, axis, 0).reshape((SWA_HKV, SWA_G, HEAD_DIM) + shape[:axis] + shape[axis + 1:])
    w = jnp.swapaxes(w, 0, 1).reshape((SWA_HQ * HEAD_DIM,) + shape[:axis] + shape[axis + 1:])
    return jnp.moveaxis(w, 0, axis)


def _even_mixer(x, batch, gains, g2, g3, layer, w_in, conv_w, conv_b, ln_g, ln_b, sinks, w_out):
    n, d = x.shape
    seq = n // batch
    ts = MIX_ROWS
    nt = seq // ts
    o_q = 2 * CONV_CH
    o_k = o_q + SWA_HQ * HEAD_DIM
    ab_in = w_in.shape[1]
    wq = _pair_heads(w_in[:, o_q:o_k], 1) * (LOG2E / math.sqrt(HEAD_DIM))
    slabs = [w_in[:, half + cb * LANES:half + (cb + 1) * LANES]
             for cb in range(CONV_CH // LANES) for half in (0, CONV_CH)]
    w_in = jnp.concatenate(slabs + [wq, w_in[:, o_k:]], axis=1).astype(BF16)
    w_out = jnp.concatenate([w_out[:CONV_CH], _pair_heads(w_out[CONV_CH:], 0)], axis=0).astype(BF16)
    bias = _swa_bias_table()
    row = pl.BlockSpec((ts, d), lambda b, i, *_: (b * nt + i, 0))

    def const(shape):
        k = len(shape)
        return pl.BlockSpec(shape, lambda b, i, *_: (0,) * k, pipeline_mode=pl.Buffered(1))

    grid_spec = pltpu.PrefetchScalarGridSpec(
        num_scalar_prefetch=1,
        grid=(batch, nt),
        in_specs=[row, _pick_spec(g2, (1, d)), _pick_spec(g3, (1, d)), const((d, ab_in)),
                  _pick_spec(layer, (CONV_K, CONV_CH)), _pick_spec(layer, (1, CONV_CH)),
                  _pick_spec(layer, (1, CONV_CH)), _pick_spec(layer, (1, CONV_CH)),
                  const(bias.shape), const((d, d))],
        out_specs=row,
        scratch_shapes=[pltpu.VMEM((CONV_CH // LANES, CONV_HALO + ts, LANES), F32),
                        pltpu.VMEM((BLK + ts, LANES), BF16),
                        pltpu.VMEM((LANES, BLK + ts), BF16),
                        pltpu.VMEM((CONV_CH // LANES, ts, LANES), F32),
                        pltpu.VMEM((ts, d), BF16)])
    return pl.pallas_call(
        _even_body,
        grid_spec=grid_spec,
        out_shape=jax.ShapeDtypeStruct((n, d), F32),
        compiler_params=pltpu.CompilerParams(
            dimension_semantics=("parallel", "arbitrary"),
            vmem_limit_bytes=48 * MIB),
        name="even_mixer",
    )(sinks * LOG2E, x, gains, gains, w_in, conv_w, conv_b, ln_g, ln_b, bias, w_out)


def _fox_proj_body(x_ref, g_ref, wqt_ref, wk_ref, wvt_ref, wf_ref, bf_ref, tri_ref, eqt_ref, ek_ref,
                   qt_ref, k_ref, vt_ref, carry_ref):
    i = pl.program_id(1)
    ts = x_ref.shape[0]

    @pl.when(i == 0)
    def _():
        carry_ref[...] = jnp.zeros(carry_ref.shape, F32)

    h = _rms(x_ref[...], g_ref[...]).astype(BF16)

    t = _dot(h, wf_ref[...]) + bf_ref[...]
    zqt = _dot_nt(wqt_ref[...], h)
    logf = jnp.minimum(t, 0.0) - jnp.log1p(jnp.exp(-jnp.abs(t)))
    tri = tri_ref[...]
    c = carry_ref[0:1, :]
    for piece in _split_bf16(logf):
        c = c + _dot(tri, piece)
    carry_ref[0:1, :] = c[ts - 1:ts, :]
    zk = _dot(h, wk_ref[...])

    c1, c2, c3 = _split_bf16(c * LOG2E)
    lane = lax.broadcasted_iota(jnp.int32, (ts, LANES), 1)
    cp = jnp.where(lane < FOX_H, c1,
                   jnp.where(lane < 2 * FOX_H, c2,
                             jnp.where(lane < 3 * FOX_H, c3,
                                       jnp.where(lane < 4 * FOX_H, 1.0, 0.0).astype(BF16))))
    aqt = _dot_nt(eqt_ref[...], cp)
    ak = _dot(cp, ek_ref[...])
    zvt = _dot_nt(wvt_ref[...], h)
    low = lane < HEAD_DIM
    for p in range(FOX_H // 2):
        pair = slice(p * LANES, (p + 1) * LANES)
        even = slice(2 * p * LANES, (2 * p + 1) * LANES)
        odd = slice((2 * p + 1) * LANES, (2 * p + 2) * LANES)
        k_ref[:, even] = jnp.where(low, zk[:, pair], ak[:, pair]).astype(BF16)
        k_ref[:, odd] = jnp.where(low, ak[:, pair], zk[:, pair]).astype(BF16)
    row = lax.broadcasted_iota(jnp.int32, (LANES, ts), 0)
    top = row < HEAD_DIM
    one_mid = (row == HEAD_DIM).astype(F32)
    one_top = (row == 0).astype(F32)
    for p in range(FOX_H // 2):
        pair = slice(p * LANES, (p + 1) * LANES)
        even = slice(2 * p * LANES, (2 * p + 1) * LANES)
        odd = slice((2 * p + 1) * LANES, (2 * p + 2) * LANES)
        vt_ref[even, :] = jnp.where(top, zvt[pair, :], one_mid).astype(BF16)
        vt_ref[odd, :] = jnp.where(top, one_top, zvt[pair, :]).astype(BF16)
        qt_ref[even, :] = jnp.where(top, zqt[pair, :], aqt[pair, :]).astype(BF16)
        qt_ref[odd, :] = jnp.where(top, aqt[pair, :], zqt[pair, :]).astype(BF16)


def _fox_placement():
    eq = np.zeros((LANES, FOX_H // 2 * LANES), np.float32)
    ek = np.zeros((LANES, FOX_H // 2 * LANES), np.float32)
    for h in range(FOX_H):
        base = (h // 2) * LANES + (HEAD_DIM if h % 2 == 0 else 0)
        for a in range(N_AUG):
            eq[a * FOX_H + h, base + a] = 1.0
            eq[N_AUG * FOX_H + h, base + N_AUG + a] = 1.0
            ek[N_AUG * FOX_H + h, base + a] = 1.0
            ek[a * FOX_H + h, base + N_AUG + a] = -1.0
    return eq, ek


def _fox_proj(x, batch, gains, g, w_in, b_f):
    n, d = x.shape
    seq = n // batch
    ts = FOX_ROWS
    nt = seq // ts
    hd = FOX_H * HEAD_DIM
    wqt = (w_in[:, :hd] * (LOG2E / math.sqrt(HEAD_DIM))).T.astype(BF16)
    wk = w_in[:, hd:2 * hd].astype(BF16)
    wvt = w_in[:, 2 * hd:3 * hd].T.astype(BF16)
    reps = N_AUG
    wf = jnp.pad(jnp.tile(w_in[:, 3 * hd:], (1, reps)), ((0, 0), (0, LANES - reps * FOX_H))).astype(BF16)
    bf = jnp.pad(jnp.tile(b_f, reps), (0, LANES - reps * FOX_H)).reshape(1, LANES).astype(F32)
    tri = jnp.asarray(np.tril(np.ones((ts, ts), np.float32)), BF16)
    eq, ek = _fox_placement()
    eqt, ek = jnp.asarray(eq.T, BF16), jnp.asarray(ek, BF16)
    row = pl.BlockSpec((ts, d), lambda b, i: (b * nt + i, 0))
    wide = pl.BlockSpec((ts, FOX_H * LANES), lambda b, i: (b * nt + i, 0))
    tall = pl.BlockSpec((None, FOX_H * LANES, ts), lambda b, i: (b, 0, i))
    return pl.pallas_call(
        _fox_proj_body,
        grid=(batch, nt),
        in_specs=[row, _pick_spec(g, (1, d)), _const_spec((hd, d)), _const_spec((d, hd)),
                  _const_spec((hd, d)), _const_spec((d, LANES)), _const_spec((1, LANES)),
                  _const_spec((ts, ts)), _const_spec(eqt.shape), _const_spec(ek.shape)],
        out_specs=[tall, wide, tall],
        out_shape=[jax.ShapeDtypeStruct((batch, FOX_H * LANES, seq), BF16),
                   jax.ShapeDtypeStruct((n, FOX_H * LANES), BF16),
                   jax.ShapeDtypeStruct((batch, FOX_H * LANES, seq), BF16)],
        scratch_shapes=[pltpu.VMEM((8, LANES), F32)],
        compiler_params=pltpu.CompilerParams(
            dimension_semantics=("parallel", "arbitrary"),
            vmem_limit_bytes=48 * MIB),
        name="fox_proj",
    )(x, gains, wqt, wk, wvt, wf, bf, tri, eqt, ek)


def _fox_attn_body(qt_ref, k_ref, vt_ref, o_ref):
    seq = k_ref.shape[0]
    tq = FOX_TQ
    tk = FOX_TK
    per_q = tq // tk
    row = lax.broadcasted_iota(jnp.int32, (LANES, tq), 0)
    key = lax.broadcasted_iota(jnp.int32, (tk, tq), 0)
    qry = lax.broadcasted_iota(jnp.int32, (tk, tq), 1)
    chains = [(qi, hh) for qi in reversed(range(seq // tq)) for hh in range(2)]

    def scores(chain, kb):
        qi, hh = chain
        head = slice(hh * LANES, (hh + 1) * LANES)
        st = _dot(k_ref[kb * tk:(kb + 1) * tk, head], qt_ref[head, qi * tq:(qi + 1) * tq])
        if kb >= qi * per_q:
            st = jnp.where(key + (kb * tk - qi * tq) <= qry, st, NEG)
        return st

    def probabilities(st, m):
        return jnp.exp2(st - m).astype(BF16)

    def weighted_values(chain, kb, pt):
        _, hh = chain
        head = slice(hh * LANES, (hh + 1) * LANES)
        return _dot(vt_ref[head, kb * tk:(kb + 1) * tk], pt)

    def blocks(stage):
        return (stage[0][0] + 1) * per_q if stage is not None else 0

    done = {}
    scored = None
    exped = None
    for cur in chains + [None, None]:
        n_cur = (cur[0] + 1) * per_q if cur is not None else 0
        sts, pts, acc = [], [], None
        for kb in range(max(n_cur, blocks(scored), blocks(exped))):
            if kb < n_cur:
                sts.append(scores(cur, kb))
            if kb < blocks(scored):
                pts.append(probabilities(scored[1][kb], scored[2]))
            if kb < blocks(exped):
                part = weighted_values(exped[0], kb, exped[1][kb])
                acc = part if acc is None else acc + part
        if exped is not None:
            qi, hh = exped[0]
            ones_row = HEAD_DIM if hh == 0 else 0
            done[hh] = acc / acc[ones_row:ones_row + 1, :]
            if hh == 1:
                out = jnp.where(row < HEAD_DIM, done[0], done[1])
                o_ref[qi * tq:(qi + 1) * tq, :] = out.T.astype(BF16)
        exped = (scored[0], pts) if scored is not None else None
        if cur is not None:
            m = functools.reduce(jnp.maximum, [jnp.max(st, axis=0, keepdims=True) for st in sts])
            scored = (cur, sts, m)
        else:
            scored = None


def _fox_attn(qt, k, vt, batch):
    n = k.shape[0]
    seq = n // batch
    pairs = FOX_H // 2
    k = k.reshape(batch, seq, FOX_H * LANES)
    pair_block = pl.BlockSpec((None, seq, 2 * LANES), lambda b, p: (b, 0, p))
    out = pl.pallas_call(
        _fox_attn_body,
        grid=(batch, pairs),
        in_specs=[pl.BlockSpec((None, 2 * LANES, seq), lambda b, p: (b, p, 0)), pair_block,
                  pl.BlockSpec((None, 2 * LANES, seq), lambda b, p: (b, p, 0))],
        out_specs=pl.BlockSpec((None, seq, LANES), lambda b, p: (b, 0, p)),
        out_shape=jax.ShapeDtypeStruct((batch, seq, FOX_H * HEAD_DIM), BF16),
        compiler_params=pltpu.CompilerParams(
            dimension_semantics=("parallel", "parallel"),
            vmem_limit_bytes=48 * MIB),
        name="fox_attn",
    )(qt, k, vt)
    return out.reshape(n, FOX_H * HEAD_DIM)


def kernel(x, norm_g, ffn_w_gate, ffn_w_up, ffn_w_down, ab_w_in, conv_w, conv_b, conv_ln_g,
           conv_ln_b, swa_sinks, ab_w_out, fox_w_in, fox_b_f, fox_w_out):
    batch, seq, d = x.shape
    depth = norm_g.shape[0]
    xf = x.reshape(batch * seq, d)
    wg, wu, wd = (w.astype(BF16) for w in (ffn_w_gate, ffn_w_up, ffn_w_down))
    n_norms = norm_g.shape[1]
    gains = norm_g.reshape(depth * n_norms, 1, d)
    conv_b, conv_ln_g, conv_ln_b = (p.reshape(p.shape[0], 1, -1) for p in (conv_b, conv_ln_g, conv_ln_b))
    for l in range(depth):
        g = l * n_norms
        xf = _ffn(xf, gains, g, g + 1, wg, wu, wd, (l, 0))
        i = l // 2
        mix = None
        if l % 2 == 0:
            xf = _even_mixer(xf, batch, gains, g + 2, g + 3, i, ab_w_in[i], conv_w, conv_b,
                             conv_ln_g, conv_ln_b, swa_sinks[i], ab_w_out[i])
        else:
            qt, k, vt = _fox_proj(xf, batch, gains, g + 2, fox_w_in[i], fox_b_f[i])
            mix = (_fox_attn(qt, k, vt, batch), fox_w_out[i], g + 3)
        xf = _ffn(xf, gains, g + 4, g + 5, wg, wu, wd, (l, 1), mix)
    return xf.reshape(batch, seq, d)
```

```python
import functools
import math

import jax
import jax.numpy as jnp
import numpy as np
from jax import lax
from jax.experimental import pallas as pl
from jax.experimental.pallas import tpu as pltpu

F32 = jnp.float32
BF16 = jnp.bfloat16

LANES = 128
V7X_VMEM_BYTES = 64 * 1024 * 1024
MIB = 1024 * 1024

D_MODEL = 1024
D_FF = 2816
EPS = 1e-6
HEAD_DIM = 64
CONV_CH = 512
CONV_K = 31
SWA_HQ = 8
SWA_HKV = 2
SWA_G = SWA_HQ // SWA_HKV
WINDOW = 128
BLK = 128
FOX_H = 16
NEG = -1e30

FFN_ROWS = 1024
FFN_SUB_ROWS = 512
FFN_CHUNK = 256
MIX_ROWS = 512
CONV_HALO = 32
CONV_PHASES = 4
FOX_ROWS = 512
FOX_TQ = 512
FOX_TK = 256
N_AUG = 3
LOG2E = 1.4426950408889634


def _rms(x, g):
    ms = jnp.mean(x * x, axis=-1, keepdims=True)
    return x * lax.rsqrt(ms + EPS) * g


def _dot(a, b):
    return jnp.dot(a, b, preferred_element_type=F32)


def _dot_nt(a, b):
    return lax.dot_general(a, b, (((1,), (1,)), ((), ())), preferred_element_type=F32)


def _split_bf16(x):
    p1 = x.astype(BF16)
    r1 = x - p1.astype(F32)
    p2 = r1.astype(BF16)
    r2 = r1 - p2.astype(F32)
    return p1, p2, r2.astype(BF16)


def _const_spec(shape):
    n = len(shape)
    return pl.BlockSpec(shape, lambda *_: (0,) * n, pipeline_mode=pl.Buffered(1))


def _pick_spec(index, shape):
    n = len(shape)
    return pl.BlockSpec((None,) + tuple(shape), lambda *_: (index,) + (0,) * n,
                        pipeline_mode=pl.Buffered(1))


def _ffn_body(*refs, mix_in):
    if mix_in:
        attn_ref, wmix_ref, gmix_ref, *refs = refs
    x_ref, gpre_ref, gpost_ref, wg_ref, wu_ref, wd_ref, o_ref, a_ref = refs
    tf = FFN_CHUNK
    for r0 in range(0, x_ref.shape[0], FFN_SUB_ROWS):
        rows = slice(r0, r0 + FFN_SUB_ROWS)
        x = x_ref[rows, :]
        if mix_in:
            x = x + _rms(_dot(attn_ref[rows, :], wmix_ref[...]), gmix_ref[...])
        h = _rms(x, gpre_ref[...]).astype(BF16)
        for c in range(wg_ref.shape[1] // tf):
            cols = slice(c * tf, (c + 1) * tf)
            g = _dot(h, wg_ref[:, cols])
            u = _dot(h, wu_ref[:, cols])
            a_ref[rows, cols] = (g * jax.nn.sigmoid(g) * u).astype(BF16)
        y = _dot(a_ref[rows, :], wd_ref[...])
        o_ref[rows, :] = x + 0.5 * _rms(y, gpost_ref[...])


def _ffn(x, gains, pre, post, wg, wu, wd, which, mix=None):
    n, d = x.shape
    tm = FFN_ROWS
    row = pl.BlockSpec((tm, d), lambda i: (i, 0))

    def stacked(rows, cols):
        return pl.BlockSpec((None, None, rows, cols), lambda i: which + (0, 0),
                            pipeline_mode=pl.Buffered(1))

    args = [x, gains, gains, wg, wu, wd]
    in_specs = [row, _pick_spec(pre, (1, d)), _pick_spec(post, (1, d)),
                stacked(d, D_FF), stacked(d, D_FF), stacked(D_FF, d)]
    weights = 3 * d * D_FF * 2
    tiles = 4 * tm * d * 4 + tm * D_FF * 2 + 4 * tm * FFN_CHUNK * 4 + 2 * tm * d * 4
    if mix is not None:
        attn, w_mix, g_mix = mix
        args = [attn, w_mix.astype(BF16), gains] + args
        in_specs = [row, _const_spec((d, d)), _pick_spec(g_mix, (1, d))] + in_specs
        weights += d * d * 2
        tiles += 2 * tm * d * 2 + tm * d * 4
    return pl.pallas_call(
        functools.partial(_ffn_body, mix_in=mix is not None),
        grid=(n // tm,),
        in_specs=in_specs,
        out_specs=row,
        out_shape=jax.ShapeDtypeStruct((n, d), F32),
        scratch_shapes=[pltpu.VMEM((tm, D_FF), BF16)],
        compiler_params=pltpu.CompilerParams(
            dimension_semantics=("parallel",),
            vmem_limit_bytes=min(V7X_VMEM_BYTES - 8 * MIB, weights + 2 * tiles)),
        name="ffn_mix" if mix is not None else "ffn",
    )(*args)


def _even_body(sinks_ref, x_ref, g2_ref, g3_ref, win_ref, cw_ref, cb_ref, lng_ref, lnb_ref,
               bias_ref, wout_ref, o_ref, a_buf, k_buf, vt_buf, y_buf, cat_buf):
    i = pl.program_id(1)
    ts = x_ref.shape[0]
    x = x_ref[...]
    h = _rms(x, g2_ref[...]).astype(BF16)
    n_slabs = CONV_CH // LANES
    chunk = 2 * LANES

    @pl.when(i == 0)
    def _():
        a_buf[:, 0:CONV_HALO, :] = jnp.zeros((n_slabs, CONV_HALO, LANES), F32)
        k_buf[0:BLK, :] = jnp.zeros((BLK, LANES), BF16)
        vt_buf[:, 0:BLK] = jnp.zeros((LANES, BLK), BF16)

    @pl.when(i > 0)
    def _():
        a_buf[:, 0:CONV_HALO, :] = a_buf[:, ts:ts + CONV_HALO, :]
        k_buf[0:BLK, :] = k_buf[ts:ts + BLK, :]
        vt_buf[:, 0:BLK] = vt_buf[:, ts:ts + BLK]

    def project(c):
        return _dot(h, win_ref[:, c * chunk:(c + 1) * chunk])

    def gate_into_slab(cb, zc):
        a_buf[cb, CONV_HALO:CONV_HALO + ts, :] = zc[:, :LANES] * jax.nn.sigmoid(zc[:, LANES:])

    first_tap = CONV_HALO - (CONV_K - 1)
    n_rows = ts // CONV_PHASES

    def conv_phase(cb, r):
        cols = slice(cb * LANES, (cb + 1) * LANES)
        acc = jnp.broadcast_to(cb_ref[:, cols], (n_rows, LANES))
        for j in range(CONV_K):
            taps = a_buf[cb, pl.ds(first_tap + r + j, n_rows, stride=CONV_PHASES), :]
            acc = acc + cw_ref[j:j + 1, cols] * taps
        y_buf[cb, pl.ds(r, n_rows, stride=CONV_PHASES), :] = acc

    assert n_slabs == ts // BLK == 4
    gate_into_slab(0, project(0))
    zq = []
    for cb in range(n_slabs - 1):
        conv_phase(cb, 0)
        gate_into_slab(cb + 1, project(cb + 1))
        conv_phase(cb, 1)
        conv_phase(cb, 2)
        if cb < 2:
            zq.append(project(n_slabs + cb))
        else:
            zkv = project(n_slabs + 2)
            k_buf[BLK:BLK + ts, :] = zkv[:, :LANES].astype(BF16)
            vt_buf[:, BLK:BLK + ts] = zkv[:, LANES:].T.astype(BF16)
        conv_phase(cb, 3)

    lane = lax.broadcasted_iota(jnp.int32, (BLK, LANES), 1)
    key_is_prev = lax.broadcasted_iota(jnp.int32, (2 * BLK, SWA_G * BLK), 0) < BLK
    units = [(qb, hk) for qb in range(ts // BLK) for hk in range(SWA_HKV)]
    last = n_slabs - 1

    def unit_scores(qb, hk):
        rows = slice(qb * BLK, (qb + 1) * BLK)
        mine = (lane < HEAD_DIM) if hk == 0 else (lane >= HEAD_DIM)
        qs = jnp.concatenate(
            [jnp.where(mine, zq[j // 2][rows, (j % 2) * LANES:(j % 2 + 1) * LANES], 0.0)
             for j in range(SWA_G)], axis=0).astype(BF16)
        st = _dot_nt(k_buf[qb * BLK:qb * BLK + 2 * BLK, :], qs)
        bias = bias_ref[hk]
        if qb == 0:
            bias = jnp.where(jnp.logical_and(i == 0, key_is_prev), NEG, bias)
        return st + bias

    def unit_softmax(hk, st):
        sink = jnp.concatenate(
            [jnp.full((1, BLK), sinks_ref[SWA_G * hk + j], F32) for j in range(SWA_G)], axis=1)
        m = jnp.maximum(jnp.max(st, axis=0, keepdims=True), sink)
        p = jnp.exp2(st - m)
        den = jnp.sum(p, axis=0, keepdims=True) + jnp.exp2(sink - m)
        return p.astype(BF16), den

    def unit_values(qb, p, den):
        return _dot(vt_buf[:, qb * BLK:qb * BLK + 2 * BLK], p) / den

    conv_phase(last, 0)
    sts = [unit_scores(qb, hk) for qb, hk in units]
    conv_phase(last, 1)
    pds = [unit_softmax(hk, st) for (qb, hk), st in zip(units, sts)]
    conv_phase(last, 2)
    ots = [unit_values(qb, p, den) for (qb, hk), (p, den) in zip(units, pds)]
    conv_phase(last, 3)
    row = lax.broadcasted_iota(jnp.int32, (LANES, SWA_G * BLK), 0)
    for qb in range(ts // BLK):
        out = jnp.where(row < HEAD_DIM, ots[SWA_HKV * qb], ots[SWA_HKV * qb + 1]).T
        for j in range(SWA_G):
            cat_buf[qb * BLK:(qb + 1) * BLK, CONV_CH + j * LANES:CONV_CH + (j + 1) * LANES] = (
                out[j * BLK:(j + 1) * BLK].astype(BF16))

    ys = [y_buf[cb] for cb in range(n_slabs)]
    mu = jnp.sum(sum(ys), axis=-1, keepdims=True) * (1.0 / CONV_CH)
    ycs = [y - mu for y in ys]
    var = jnp.sum(sum(yc * yc for yc in ycs), axis=-1, keepdims=True) * (1.0 / CONV_CH)
    inv = lax.rsqrt(var + EPS)
    for cb in range(n_slabs):
        cols = slice(cb * LANES, (cb + 1) * LANES)
        yn = ycs[cb] * inv * lng_ref[:, cols] + lnb_ref[:, cols]
        cat_buf[:, cols] = (yn * jax.nn.sigmoid(yn)).astype(BF16)

    y2 = _dot(cat_buf[...], wout_ref[...])
    o_ref[...] = x + _rms(y2, g3_ref[...])


def _swa_bias_table():
    t_loc = np.arange(BLK)[None, :]
    s_loc = np.arange(2 * BLK)[:, None]
    dist = t_loc + BLK - s_loc
    valid = (dist >= 0) & (dist < WINDOW)
    slopes = [2.0 ** (-8.0 * (g + 1) / SWA_HQ) for g in range(SWA_HQ)]
    per_head = [np.where(valid, -np.float32(sl * LOG2E) * dist.astype(np.float32), np.float32(NEG))
                for sl in slopes]
    return np.stack([np.concatenate(per_head[SWA_G * hk:SWA_G * (hk + 1)], axis=1)
                     for hk in range(SWA_HKV)]).astype(np.float32)


def _pair_heads(w, axis):
    shape = w.shape
    w = jnp.moveaxis(w, axis, 0).reshape((SWA_HKV, SWA_G, HEAD_DIM) + shape[:axis] + shape[axis + 1:])
    w = jnp.swapaxes(w, 0, 1).reshape((SWA_HQ * HEAD_DIM,) + shape[:axis] + shape[axis + 1:])
    return jnp.moveaxis(w, 0, axis)


def _even_mixer(x, batch, gains, g2, g3, layer, w_in, conv_w, conv_b, ln_g, ln_b, sinks, w_out):
    n, d = x.shape
    seq = n // batch
    ts = MIX_ROWS
    nt = seq // ts
    o_q = 2 * CONV_CH
    o_k = o_q + SWA_HQ * HEAD_DIM
    ab_in = w_in.shape[1]
    wq = _pair_heads(w_in[:, o_q:o_k], 1) * (LOG2E / math.sqrt(HEAD_DIM))
    slabs = [w_in[:, half + cb * LANES:half + (cb + 1) * LANES]
             for cb in range(CONV_CH // LANES) for half in (0, CONV_CH)]
    w_in = jnp.concatenate(slabs + [wq, w_in[:, o_k:]], axis=1).astype(BF16)
    w_out = jnp.concatenate([w_out[:CONV_CH], _pair_heads(w_out[CONV_CH:], 0)], axis=0).astype(BF16)
    bias = _swa_bias_table()
    row = pl.BlockSpec((ts, d), lambda b, i, *_: (b * nt + i, 0))

    def const(shape):
        k = len(shape)
        return pl.BlockSpec(shape, lambda b, i, *_: (0,) * k, pipeline_mode=pl.Buffered(1))

    grid_spec = pltpu.PrefetchScalarGridSpec(
        num_scalar_prefetch=1,
        grid=(batch, nt),
        in_specs=[row, _pick_spec(g2, (1, d)), _pick_spec(g3, (1, d)), const((d, ab_in)),
                  _pick_spec(layer, (CONV_K, CONV_CH)), _pick_spec(layer, (1, CONV_CH)),
                  _pick_spec(layer, (1, CONV_CH)), _pick_spec(layer, (1, CONV_CH)),
                  const(bias.shape), const((d, d))],
        out_specs=row,
        scratch_shapes=[pltpu.VMEM((CONV_CH // LANES, CONV_HALO + ts, LANES), F32),
                        pltpu.VMEM((BLK + ts, LANES), BF16),
                        pltpu.VMEM((LANES, BLK + ts), BF16),
                        pltpu.VMEM((CONV_CH // LANES, ts, LANES), F32),
                        pltpu.VMEM((ts, d), BF16)])
    return pl.pallas_call(
        _even_body,
        grid_spec=grid_spec,
        out_shape=jax.ShapeDtypeStruct((n, d), F32),
        compiler_params=pltpu.CompilerParams(
            dimension_semantics=("parallel", "arbitrary"),
            vmem_limit_bytes=48 * MIB),
        name="even_mixer",
    )(sinks * LOG2E, x, gains, gains, w_in, conv_w, conv_b, ln_g, ln_b, bias, w_out)


def _fox_proj_body(x_ref, g_ref, wqt_ref, wk_ref, wvt_ref, wf_ref, bf_ref, tri_ref, eqt_ref, ek_ref,
                   qt_ref, k_ref, vt_ref, carry_ref):
    i = pl.program_id(1)
    ts = x_ref.shape[0]

    @pl.when(i == 0)
    def _():
        carry_ref[...] = jnp.zeros(carry_ref.shape, F32)

    h = _rms(x_ref[...], g_ref[...]).astype(BF16)

    t = _dot(h, wf_ref[...]) + bf_ref[...]
    zqt = _dot_nt(wqt_ref[...], h)
    logf = jnp.minimum(t, 0.0) - jnp.log1p(jnp.exp(-jnp.abs(t)))
    tri = tri_ref[...]
    half = ts // 2
    total = carry_ref[0:1, :]
    halves = []
    for r0 in (0, half):
        ch = total
        for piece in _split_bf16(logf[r0:r0 + half]):
            ch = ch + _dot(tri, piece)
        total = ch[half - 1:half, :]
        halves.append(ch)
    c = jnp.concatenate(halves, axis=0)
    carry_ref[0:1, :] = total
    zk = _dot(h, wk_ref[...])

    c1, c2, c3 = _split_bf16(c * LOG2E)
    lane = lax.broadcasted_iota(jnp.int32, (ts, LANES), 1)
    cp = jnp.where(lane < FOX_H, c1,
                   jnp.where(lane < 2 * FOX_H, c2,
                             jnp.where(lane < 3 * FOX_H, c3,
                                       jnp.where(lane < 4 * FOX_H, 1.0, 0.0).astype(BF16))))
    aqt = _dot_nt(eqt_ref[...], cp)
    ak = _dot(cp, ek_ref[...])
    zvt = _dot_nt(wvt_ref[...], h)
    low = lane < HEAD_DIM
    for p in range(FOX_H // 2):
        pair = slice(p * LANES, (p + 1) * LANES)
        even = slice(2 * p * LANES, (2 * p + 1) * LANES)
        odd = slice((2 * p + 1) * LANES, (2 * p + 2) * LANES)
        k_ref[:, even] = jnp.where(low, zk[:, pair], ak[:, pair]).astype(BF16)
        k_ref[:, odd] = jnp.where(low, ak[:, pair], zk[:, pair]).astype(BF16)
    row = lax.broadcasted_iota(jnp.int32, (LANES, ts), 0)
    top = row < HEAD_DIM
    one_mid = (row == HEAD_DIM).astype(F32)
    one_top = (row == 0).astype(F32)
    for p in range(FOX_H // 2):
        pair = slice(p * LANES, (p + 1) * LANES)
        even = slice(2 * p * LANES, (2 * p + 1) * LANES)
        odd = slice((2 * p + 1) * LANES, (2 * p + 2) * LANES)
        vt_ref[even, :] = jnp.where(top, zvt[pair, :], one_mid).astype(BF16)
        vt_ref[odd, :] = jnp.where(top, one_top, zvt[pair, :]).astype(BF16)
        qt_ref[even, :] = jnp.where(top, zqt[pair, :], aqt[pair, :]).astype(BF16)
        qt_ref[odd, :] = jnp.where(top, aqt[pair, :], zqt[pair, :]).astype(BF16)


def _fox_placement():
    eq = np.zeros((LANES, FOX_H // 2 * LANES), np.float32)
    ek = np.zeros((LANES, FOX_H // 2 * LANES), np.float32)
    for h in range(FOX_H):
        base = (h // 2) * LANES + (HEAD_DIM if h % 2 == 0 else 0)
        for a in range(N_AUG):
            eq[a * FOX_H + h, base + a] = 1.0
            eq[N_AUG * FOX_H + h, base + N_AUG + a] = 1.0
            ek[N_AUG * FOX_H + h, base + a] = 1.0
            ek[a * FOX_H + h, base + N_AUG + a] = -1.0
    return eq, ek


def _fox_proj(x, batch, gains, g, w_in, b_f):
    n, d = x.shape
    seq = n // batch
    ts = FOX_ROWS
    nt = seq // ts
    hd = FOX_H * HEAD_DIM
    wqt = (w_in[:, :hd] * (LOG2E / math.sqrt(HEAD_DIM))).T.astype(BF16)
    wk = w_in[:, hd:2 * hd].astype(BF16)
    wvt = w_in[:, 2 * hd:3 * hd].T.astype(BF16)
    reps = N_AUG
    wf = jnp.pad(jnp.tile(w_in[:, 3 * hd:], (1, reps)), ((0, 0), (0, LANES - reps * FOX_H))).astype(BF16)
    bf = jnp.pad(jnp.tile(b_f, reps), (0, LANES - reps * FOX_H)).reshape(1, LANES).astype(F32)
    tri = jnp.asarray(np.tril(np.ones((ts // 2, ts // 2), np.float32)), BF16)
    eq, ek = _fox_placement()
    eqt, ek = jnp.asarray(eq.T, BF16), jnp.asarray(ek, BF16)
    row = pl.BlockSpec((ts, d), lambda b, i: (b * nt + i, 0))
    wide = pl.BlockSpec((ts, FOX_H * LANES), lambda b, i: (b * nt + i, 0))
    tall = pl.BlockSpec((None, FOX_H * LANES, ts), lambda b, i: (b, 0, i))
    return pl.pallas_call(
        _fox_proj_body,
        grid=(batch, nt),
        in_specs=[row, _pick_spec(g, (1, d)), _const_spec((hd, d)), _const_spec((d, hd)),
                  _const_spec((hd, d)), _const_spec((d, LANES)), _const_spec((1, LANES)),
                  _const_spec(tri.shape), _const_spec(eqt.shape), _const_spec(ek.shape)],
        out_specs=[tall, wide, tall],
        out_shape=[jax.ShapeDtypeStruct((batch, FOX_H * LANES, seq), BF16),
                   jax.ShapeDtypeStruct((n, FOX_H * LANES), BF16),
                   jax.ShapeDtypeStruct((batch, FOX_H * LANES, seq), BF16)],
        scratch_shapes=[pltpu.VMEM((8, LANES), F32)],
        compiler_params=pltpu.CompilerParams(
            dimension_semantics=("parallel", "arbitrary"),
            vmem_limit_bytes=48 * MIB),
        name="fox_proj",
    )(x, gains, wqt, wk, wvt, wf, bf, tri, eqt, ek)


def _fox_attn_body(qt_ref, k_ref, vt_ref, o_ref):
    seq = k_ref.shape[0]
    tq = FOX_TQ
    tk = FOX_TK
    per_q = tq // tk
    row = lax.broadcasted_iota(jnp.int32, (LANES, tq), 0)
    chains = [(qi, hh) for qi in reversed(range(seq // tq)) for hh in range(2)]

    def queries_of(chain, kb):
        skip = tq - tk if kb == (chain[0] + 1) * per_q - 1 else 0
        return skip, tq - skip

    def scores(chain, kb):
        qi, hh = chain
        head = slice(hh * LANES, (hh + 1) * LANES)
        skip, nq = queries_of(chain, kb)
        q0 = qi * tq + skip
        st = _dot(k_ref[kb * tk:(kb + 1) * tk, head], qt_ref[head, q0:q0 + nq])
        if kb >= qi * per_q:
            key = lax.broadcasted_iota(jnp.int32, (tk, nq), 0)
            qry = lax.broadcasted_iota(jnp.int32, (tk, nq), 1)
            st = jnp.where(key + (kb * tk - q0) <= qry, st, NEG)
        return st

    def column_max(chain, sts):
        m = None
        for kb, st in enumerate(sts):
            skip, _ = queries_of(chain, kb)
            mb = jnp.max(st, axis=0, keepdims=True)
            if skip:
                mb = jnp.concatenate([m[:, :skip], jnp.maximum(m[:, skip:], mb)], axis=1)
                m = mb
            else:
                m = mb if m is None else jnp.maximum(m, mb)
        return m

    def probabilities(chain, kb, st, m):
        skip, _ = queries_of(chain, kb)
        return jnp.exp2(st - m[:, skip:]).astype(BF16)

    def weighted_values(chain, kb, pt, acc):
        _, hh = chain
        head = slice(hh * LANES, (hh + 1) * LANES)
        skip, _ = queries_of(chain, kb)
        part = _dot(vt_ref[head, kb * tk:(kb + 1) * tk], pt)
        if acc is None:
            return part
        if skip:
            return jnp.concatenate([acc[:, :skip], acc[:, skip:] + part], axis=1)
        return acc + part

    def blocks(stage):
        return (stage[0][0] + 1) * per_q if stage is not None else 0

    done = {}
    scored = None
    exped = None
    for cur in chains + [None, None]:
        n_cur = (cur[0] + 1) * per_q if cur is not None else 0
        sts, pts, acc = [], [], None
        for kb in range(max(n_cur, blocks(scored), blocks(exped))):
            if kb < n_cur:
                sts.append(scores(cur, kb))
            if kb < blocks(scored):
                pts.append(probabilities(scored[0], kb, scored[1][kb], scored[2]))
            if kb < blocks(exped):
                acc = weighted_values(exped[0], kb, exped[1][kb], acc)
        if exped is not None:
            qi, hh = exped[0]
            ones_row = HEAD_DIM if hh == 0 else 0
            done[hh] = acc / acc[ones_row:ones_row + 1, :]
            if hh == 1:
                out = jnp.where(row < HEAD_DIM, done[0], done[1])
                o_ref[qi * tq:(qi + 1) * tq, :] = out.T.astype(BF16)
        exped = (scored[0], pts) if scored is not None else None
        scored = (cur, sts, column_max(cur, sts)) if cur is not None else None


def _fox_attn(qt, k, vt, batch):
    n = k.shape[0]
    seq = n // batch
    pairs = FOX_H // 2
    k = k.reshape(batch, seq, FOX_H * LANES)
    pair_block = pl.BlockSpec((None, seq, 2 * LANES), lambda b, p: (b, 0, p))
    out = pl.pallas_call(
        _fox_attn_body,
        grid=(batch, pairs),
        in_specs=[pl.BlockSpec((None, 2 * LANES, seq), lambda b, p: (b, p, 0)), pair_block,
                  pl.BlockSpec((None, 2 * LANES, seq), lambda b, p: (b, p, 0))],
        out_specs=pl.BlockSpec((None, seq, LANES), lambda b, p: (b, 0, p)),
        out_shape=jax.ShapeDtypeStruct((batch, seq, FOX_H * HEAD_DIM), BF16),
        compiler_params=pltpu.CompilerParams(
            dimension_semantics=("parallel", "parallel"),
            vmem_limit_bytes=48 * MIB),
        name="fox_attn",
    )(qt, k, vt)
    return out.reshape(n, FOX_H * HEAD_DIM)


def kernel(x, norm_g, ffn_w_gate, ffn_w_up, ffn_w_down, ab_w_in, conv_w, conv_b, conv_ln_g,
           conv_ln_b, swa_sinks, ab_w_out, fox_w_in, fox_b_f, fox_w_out):
    batch, seq, d = x.shape
    depth = norm_g.shape[0]
    xf = x.reshape(batch * seq, d)
    wg, wu, wd = (w.astype(BF16) for w in (ffn_w_gate, ffn_w_up, ffn_w_down))
    n_norms = norm_g.shape[1]
    gains = norm_g.reshape(depth * n_norms, 1, d)
    conv_b, conv_ln_g, conv_ln_b = (p.reshape(p.shape[0], 1, -1) for p in (conv_b, conv_ln_g, conv_ln_b))
    for l in range(depth):
        g = l * n_norms
        xf = _ffn(xf, gains, g, g + 1, wg, wu, wd, (l, 0))
        i = l // 2
        mix = None
        if l % 2 == 0:
            xf = _even_mixer(xf, batch, gains, g + 2, g + 3, i, ab_w_in[i], conv_w, conv_b,
                             conv_ln_g, conv_ln_b, swa_sinks[i], ab_w_out[i])
        else:
            qt, k, vt = _fox_proj(xf, batch, gains, g + 2, fox_w_in[i], fox_b_f[i])
            mix = (_fox_attn(qt, k, vt, batch), fox_w_out[i], g + 3)
        xf = _ffn(xf, gains, g + 4, g + 5, wg, wu, wd, (l, 1), mix)
    return xf.reshape(batch, seq, d)
```

```python
import functools
import math

import jax
import jax.numpy as jnp
import numpy as np
from jax import lax
from jax.experimental import pallas as pl
from jax.experimental.pallas import tpu as pltpu

F32 = jnp.float32
BF16 = jnp.bfloat16

LANES = 128
V7X_VMEM_BYTES = 64 * 1024 * 1024
MIB = 1024 * 1024

D_MODEL = 1024
D_FF = 2816
EPS = 1e-6
HEAD_DIM = 64
CONV_CH = 512
CONV_K = 31
SWA_HQ = 8
SWA_HKV = 2
SWA_G = SWA_HQ // SWA_HKV
WINDOW = 128
BLK = 128
FOX_H = 16
NEG = -1e30

FFN_ROWS = 1024
FFN_SUB_ROWS = 512
FFN_CHUNK = 256
MIX_ROWS = 512
CONV_HALO = 32
CONV_PHASES = 4
FOX_ROWS = 512
FOX_TQ = 512
FOX_TK = 256
N_AUG = 3
LOG2E = 1.4426950408889634


def _rms(x, g):
    ms = jnp.mean(x * x, axis=-1, keepdims=True)
    return x * lax.rsqrt(ms + EPS) * g


def _dot(a, b):
    return jnp.dot(a, b, preferred_element_type=F32)


def _dot_nt(a, b):
    return lax.dot_general(a, b, (((1,), (1,)), ((), ())), preferred_element_type=F32)


def _split_bf16(x):
    p1 = x.astype(BF16)
    r1 = x - p1.astype(F32)
    p2 = r1.astype(BF16)
    r2 = r1 - p2.astype(F32)
    return p1, p2, r2.astype(BF16)


def _const_spec(shape):
    n = len(shape)
    return pl.BlockSpec(shape, lambda *_: (0,) * n, pipeline_mode=pl.Buffered(1))


def _pick_spec(index, shape):
    n = len(shape)
    return pl.BlockSpec((None,) + tuple(shape), lambda *_: (index,) + (0,) * n,
                        pipeline_mode=pl.Buffered(1))


def _ffn_body(*refs, mix_in, cast_next):
    if mix_in:
        attn_ref, wmix_ref, gmix_ref, *refs = refs
    if cast_next:
        (x_ref, gpre_ref, gpost_ref, wg_ref, wu_ref, wd_ref, ng32_ref, nu32_ref, nd32_ref,
         o_ref, ng_ref, nu_ref, nd_ref, a_ref) = refs
        casts = [(ng32_ref, ng_ref), (nu32_ref, nu_ref), (nd32_ref, nd_ref)]
    else:
        casts = []
        x_ref, gpre_ref, gpost_ref, wg_ref, wu_ref, wd_ref, o_ref, a_ref = refs
    tf = FFN_CHUNK
    for r0 in range(0, x_ref.shape[0], FFN_SUB_ROWS):
        rows = slice(r0, r0 + FFN_SUB_ROWS)
        x = x_ref[rows, :]
        if mix_in:
            x = x + _rms(_dot(attn_ref[rows, :], wmix_ref[...]), gmix_ref[...])
        h = _rms(x, gpre_ref[...]).astype(BF16)
        for c in range(wg_ref.shape[1] // tf):
            if casts and c % 2 == 1:
                src, dst = casts.pop()
                dst[...] = src[...].astype(BF16)
            cols = slice(c * tf, (c + 1) * tf)
            g = _dot(h, wg_ref[:, cols])
            u = _dot(h, wu_ref[:, cols])
            a_ref[rows, cols] = (g * jax.nn.sigmoid(g) * u).astype(BF16)
        y = _dot(a_ref[rows, :], wd_ref[...])
        o_ref[rows, :] = x + 0.5 * _rms(y, gpost_ref[...])


def _ffn(x, gains, pre, post, weights_bf16, mix=None, cast=None):
    n, d = x.shape
    tm = FFN_ROWS
    steps = n // tm
    row = pl.BlockSpec((tm, d), lambda i: (i, 0))
    args = [x, gains, gains, *weights_bf16]
    in_specs = [row, _pick_spec(pre, (1, d)), _pick_spec(post, (1, d)),
                _const_spec((d, D_FF)), _const_spec((d, D_FF)), _const_spec((D_FF, d))]
    out_specs = [row]
    out_shape = [jax.ShapeDtypeStruct((n, d), F32)]
    weights = 3 * d * D_FF * 2
    tiles = 4 * tm * d * 4 + tm * D_FF * 2 + 4 * tm * FFN_CHUNK * 4 + 2 * tm * d * 4
    if cast is not None:
        stacks, which = cast
        for w in stacks:
            rows, cols = w.shape[2] // steps, w.shape[3]
            args.append(w)
            in_specs.append(pl.BlockSpec((None, None, rows, cols), lambda i: which + (i, 0)))
            out_specs.append(pl.BlockSpec((rows, cols), lambda i: (i, 0)))
            out_shape.append(jax.ShapeDtypeStruct(w.shape[2:], BF16))
            tiles += rows * cols * (4 + 2)
    if mix is not None:
        attn, w_mix, g_mix = mix
        args = [attn, w_mix.astype(BF16), gains] + args
        in_specs = [row, _const_spec((d, d)), _pick_spec(g_mix, (1, d))] + in_specs
        weights += d * d * 2
        tiles += 2 * tm * d * 2 + tm * d * 4
    out = pl.pallas_call(
        functools.partial(_ffn_body, mix_in=mix is not None, cast_next=cast is not None),
        grid=(steps,),
        in_specs=in_specs,
        out_specs=out_specs,
        out_shape=out_shape,
        scratch_shapes=[pltpu.VMEM((tm, D_FF), BF16)],
        compiler_params=pltpu.CompilerParams(
            dimension_semantics=("parallel",),
            vmem_limit_bytes=min(V7X_VMEM_BYTES - 8 * MIB, weights + 2 * tiles)),
        name="ffn_mix" if mix is not None else "ffn",
    )(*args)
    return out[0], tuple(out[1:])


def _even_body(sinks_ref, x_ref, g2_ref, g3_ref, win_ref, cw_ref, cb_ref, lng_ref, lnb_ref,
               bias_ref, wout_ref, o_ref, a_buf, k_buf, vt_buf, y_buf, cat_buf):
    i = pl.program_id(1)
    ts = x_ref.shape[0]
    x = x_ref[...]
    h = _rms(x, g2_ref[...]).astype(BF16)
    n_slabs = CONV_CH // LANES
    chunk = 2 * LANES

    @pl.when(i == 0)
    def _():
        a_buf[:, 0:CONV_HALO, :] = jnp.zeros((n_slabs, CONV_HALO, LANES), F32)
        k_buf[0:BLK, :] = jnp.zeros((BLK, LANES), BF16)
        vt_buf[:, 0:BLK] = jnp.zeros((LANES, BLK), BF16)

    @pl.when(i > 0)
    def _():
        a_buf[:, 0:CONV_HALO, :] = a_buf[:, ts:ts + CONV_HALO, :]
        k_buf[0:BLK, :] = k_buf[ts:ts + BLK, :]
        vt_buf[:, 0:BLK] = vt_buf[:, ts:ts + BLK]

    def project(c):
        return _dot(h, win_ref[:, c * chunk:(c + 1) * chunk])

    def gate_into_slab(cb, zc):
        a_buf[cb, CONV_HALO:CONV_HALO + ts, :] = zc[:, :LANES] * jax.nn.sigmoid(zc[:, LANES:])

    first_tap = CONV_HALO - (CONV_K - 1)
    n_rows = ts // CONV_PHASES

    def conv_phase(cb, r):
        cols = slice(cb * LANES, (cb + 1) * LANES)
        acc = jnp.broadcast_to(cb_ref[:, cols], (n_rows, LANES))
        for j in range(CONV_K):
            taps = a_buf[cb, pl.ds(first_tap + r + j, n_rows, stride=CONV_PHASES), :]
            acc = acc + cw_ref[j:j + 1, cols] * taps
        y_buf[cb, pl.ds(r, n_rows, stride=CONV_PHASES), :] = acc

    assert n_slabs == ts // BLK == 4
    gate_into_slab(0, project(0))
    zq = []
    for cb in range(n_slabs - 1):
        conv_phase(cb, 0)
        gate_into_slab(cb + 1, project(cb + 1))
        conv_phase(cb, 1)
        conv_phase(cb, 2)
        if cb < 2:
            zq.append(project(n_slabs + cb))
        else:
            zkv = project(n_slabs + 2)
            k_buf[BLK:BLK + ts, :] = zkv[:, :LANES].astype(BF16)
            vt_buf[:, BLK:BLK + ts] = zkv[:, LANES:].T.astype(BF16)
        conv_phase(cb, 3)

    lane = lax.broadcasted_iota(jnp.int32, (BLK, LANES), 1)
    key_is_prev = lax.broadcasted_iota(jnp.int32, (2 * BLK, SWA_G * BLK), 0) < BLK
    units = [(qb, hk) for qb in range(ts // BLK) for hk in range(SWA_HKV)]
    last = n_slabs - 1

    def unit_scores(qb, hk):
        rows = slice(qb * BLK, (qb + 1) * BLK)
        mine = (lane < HEAD_DIM) if hk == 0 else (lane >= HEAD_DIM)
        qs = jnp.concatenate(
            [jnp.where(mine, zq[j // 2][rows, (j % 2) * LANES:(j % 2 + 1) * LANES], 0.0)
             for j in range(SWA_G)], axis=0).astype(BF16)
        st = _dot_nt(k_buf[qb * BLK:qb * BLK + 2 * BLK, :], qs)
        bias = bias_ref[hk]
        if qb == 0:
            bias = jnp.where(jnp.logical_and(i == 0, key_is_prev), NEG, bias)
        return st + bias

    def unit_softmax(hk, st):
        sink = jnp.concatenate(
            [jnp.full((1, BLK), sinks_ref[SWA_G * hk + j], F32) for j in range(SWA_G)], axis=1)
        m = jnp.maximum(jnp.max(st, axis=0, keepdims=True), sink)
        p = jnp.exp2(st - m)
        den = jnp.sum(p, axis=0, keepdims=True) + jnp.exp2(sink - m)
        return p.astype(BF16), den

    def unit_values(qb, p, den):
        return _dot(vt_buf[:, qb * BLK:qb * BLK + 2 * BLK], p) / den

    conv_phase(last, 0)
    sts = [unit_scores(qb, hk) for qb, hk in units]
    conv_phase(last, 1)
    pds = [unit_softmax(hk, st) for (qb, hk), st in zip(units, sts)]
    conv_phase(last, 2)
    ots = [unit_values(qb, p, den) for (qb, hk), (p, den) in zip(units, pds)]
    conv_phase(last, 3)
    row = lax.broadcasted_iota(jnp.int32, (LANES, SWA_G * BLK), 0)
    for qb in range(ts // BLK):
        out = jnp.where(row < HEAD_DIM, ots[SWA_HKV * qb], ots[SWA_HKV * qb + 1]).T
        for j in range(SWA_G):
            cat_buf[qb * BLK:(qb + 1) * BLK, CONV_CH + j * LANES:CONV_CH + (j + 1) * LANES] = (
                out[j * BLK:(j + 1) * BLK].astype(BF16))

    ys = [y_buf[cb] for cb in range(n_slabs)]
    mu = jnp.sum(sum(ys), axis=-1, keepdims=True) * (1.0 / CONV_CH)
    ycs = [y - mu for y in ys]
    var = jnp.sum(sum(yc * yc for yc in ycs), axis=-1, keepdims=True) * (1.0 / CONV_CH)
    inv = lax.rsqrt(var + EPS)
    for cb in range(n_slabs):
        cols = slice(cb * LANES, (cb + 1) * LANES)
        yn = ycs[cb] * inv * lng_ref[:, cols] + lnb_ref[:, cols]
        cat_buf[:, cols] = (yn * jax.nn.sigmoid(yn)).astype(BF16)

    y2 = _dot(cat_buf[...], wout_ref[...])
    o_ref[...] = x + _rms(y2, g3_ref[...])


def _swa_bias_table():
    t_loc = np.arange(BLK)[None, :]
    s_loc = np.arange(2 * BLK)[:, None]
    dist = t_loc + BLK - s_loc
    valid = (dist >= 0) & (dist < WINDOW)
    slopes = [2.0 ** (-8.0 * (g + 1) / SWA_HQ) for g in range(SWA_HQ)]
    per_head = [np.where(valid, -np.float32(sl * LOG2E) * dist.astype(np.float32), np.float32(NEG))
                for sl in slopes]
    return np.stack([np.concatenate(per_head[SWA_G * hk:SWA_G * (hk + 1)], axis=1)
                     for hk in range(SWA_HKV)]).astype(np.float32)


def _pair_heads(w, axis):
    shape = w.shape
    w = jnp.moveaxis(w, axis, 0).reshape((SWA_HKV, SWA_G, HEAD_DIM) + shape[:axis] + shape[axis + 1:])
    w = jnp.swapaxes(w, 0, 1).reshape((SWA_HQ * HEAD_DIM,) + shape[:axis] + shape[axis + 1:])
    return jnp.moveaxis(w, 0, axis)


def _even_mixer(x, batch, gains, g2, g3, layer, w_in, conv_w, conv_b, ln_g, ln_b, sinks, w_out):
    n, d = x.shape
    seq = n // batch
    ts = MIX_ROWS
    nt = seq // ts
    o_q = 2 * CONV_CH
    o_k = o_q + SWA_HQ * HEAD_DIM
    ab_in = w_in.shape[1]
    wq = _pair_heads(w_in[:, o_q:o_k], 1) * (LOG2E / math.sqrt(HEAD_DIM))
    slabs = [w_in[:, half + cb * LANES:half + (cb + 1) * LANES]
             for cb in range(CONV_CH // LANES) for half in (0, CONV_CH)]
    w_in = jnp.concatenate(slabs + [wq, w_in[:, o_k:]], axis=1).astype(BF16)
    w_out = jnp.concatenate([w_out[:CONV_CH], _pair_heads(w_out[CONV_CH:], 0)], axis=0).astype(BF16)
    bias = _swa_bias_table()
    row = pl.BlockSpec((ts, d), lambda b, i, *_: (b * nt + i, 0))

    def const(shape):
        k = len(shape)
        return pl.BlockSpec(shape, lambda b, i, *_: (0,) * k, pipeline_mode=pl.Buffered(1))

    grid_spec = pltpu.PrefetchScalarGridSpec(
        num_scalar_prefetch=1,
        grid=(batch, nt),
        in_specs=[row, _pick_spec(g2, (1, d)), _pick_spec(g3, (1, d)), const((d, ab_in)),
                  _pick_spec(layer, (CONV_K, CONV_CH)), _pick_spec(layer, (1, CONV_CH)),
                  _pick_spec(layer, (1, CONV_CH)), _pick_spec(layer, (1, CONV_CH)),
                  const(bias.shape), const((d, d))],
        out_specs=row,
        scratch_shapes=[pltpu.VMEM((CONV_CH // LANES, CONV_HALO + ts, LANES), F32),
                        pltpu.VMEM((BLK + ts, LANES), BF16),
                        pltpu.VMEM((LANES, BLK + ts), BF16),
                        pltpu.VMEM((CONV_CH // LANES, ts, LANES), F32),
                        pltpu.VMEM((ts, d), BF16)])
    return pl.pallas_call(
        _even_body,
        grid_spec=grid_spec,
        out_shape=jax.ShapeDtypeStruct((n, d), F32),
        compiler_params=pltpu.CompilerParams(
            dimension_semantics=("parallel", "arbitrary"),
            vmem_limit_bytes=48 * MIB),
        name="even_mixer",
    )(sinks * LOG2E, x, gains, gains, w_in, conv_w, conv_b, ln_g, ln_b, bias, w_out)


def _fox_proj_body(x_ref, g_ref, wqt_ref, wk_ref, wvt_ref, wf_ref, bf_ref, tri_ref, eqt_ref, ek_ref,
                   qt_ref, k_ref, vt_ref, carry_ref):
    i = pl.program_id(1)
    ts = x_ref.shape[0]

    @pl.when(i == 0)
    def _():
        carry_ref[...] = jnp.zeros(carry_ref.shape, F32)

    h = _rms(x_ref[...], g_ref[...]).astype(BF16)

    t = _dot(h, wf_ref[...]) + bf_ref[...]
    zqt = _dot_nt(wqt_ref[...], h)
    logf = jnp.minimum(t, 0.0) - jnp.log1p(jnp.exp(-jnp.abs(t)))
    tri = tri_ref[...]
    half = ts // 2
    total = carry_ref[0:1, :]
    halves = []
    for r0 in (0, half):
        ch = total
        for piece in _split_bf16(logf[r0:r0 + half]):
            ch = ch + _dot(tri, piece)
        total = ch[half - 1:half, :]
        halves.append(ch)
    c = jnp.concatenate(halves, axis=0)
    carry_ref[0:1, :] = total
    zk = _dot(h, wk_ref[...])

    c1, c2, c3 = _split_bf16(c * LOG2E)
    lane = lax.broadcasted_iota(jnp.int32, (ts, LANES), 1)
    cp = jnp.where(lane < FOX_H, c1,
                   jnp.where(lane < 2 * FOX_H, c2,
                             jnp.where(lane < 3 * FOX_H, c3,
                                       jnp.where(lane < 4 * FOX_H, 1.0, 0.0).astype(BF16))))
    aqt = _dot_nt(eqt_ref[...], cp)
    ak = _dot(cp, ek_ref[...])
    zvt = _dot_nt(wvt_ref[...], h)
    low = lane < HEAD_DIM
    for p in range(FOX_H // 2):
        pair = slice(p * LANES, (p + 1) * LANES)
        even = slice(2 * p * LANES, (2 * p + 1) * LANES)
        odd = slice((2 * p + 1) * LANES, (2 * p + 2) * LANES)
        k_ref[:, even] = jnp.where(low, zk[:, pair], ak[:, pair]).astype(BF16)
        k_ref[:, odd] = jnp.where(low, ak[:, pair], zk[:, pair]).astype(BF16)
    row = lax.broadcasted_iota(jnp.int32, (LANES, ts), 0)
    top = row < HEAD_DIM
    one_mid = (row == HEAD_DIM).astype(F32)
    one_top = (row == 0).astype(F32)
    for p in range(FOX_H // 2):
        pair = slice(p * LANES, (p + 1) * LANES)
        even = slice(2 * p * LANES, (2 * p + 1) * LANES)
        odd = slice((2 * p + 1) * LANES, (2 * p + 2) * LANES)
        vt_ref[even, :] = jnp.where(top, zvt[pair, :], one_mid).astype(BF16)
        vt_ref[odd, :] = jnp.where(top, one_top, zvt[pair, :]).astype(BF16)
        qt_ref[even, :] = jnp.where(top, zqt[pair, :], aqt[pair, :]).astype(BF16)
        qt_ref[odd, :] = jnp.where(top, aqt[pair, :], zqt[pair, :]).astype(BF16)


def _fox_placement():
    eq = np.zeros((LANES, FOX_H // 2 * LANES), np.float32)
    ek = np.zeros((LANES, FOX_H // 2 * LANES), np.float32)
    for h in range(FOX_H):
        base = (h // 2) * LANES + (HEAD_DIM if h % 2 == 0 else 0)
        for a in range(N_AUG):
            eq[a * FOX_H + h, base + a] = 1.0
            eq[N_AUG * FOX_H + h, base + N_AUG + a] = 1.0
            ek[N_AUG * FOX_H + h, base + a] = 1.0
            ek[a * FOX_H + h, base + N_AUG + a] = -1.0
    return eq, ek


def _fox_proj(x, batch, gains, g, w_in, b_f):
    n, d = x.shape
    seq = n // batch
    ts = FOX_ROWS
    nt = seq // ts
    hd = FOX_H * HEAD_DIM
    wqt = (w_in[:, :hd] * (LOG2E / math.sqrt(HEAD_DIM))).T.astype(BF16)
    wk = w_in[:, hd:2 * hd].astype(BF16)
    wvt = w_in[:, 2 * hd:3 * hd].T.astype(BF16)
    reps = N_AUG
    wf = jnp.pad(jnp.tile(w_in[:, 3 * hd:], (1, reps)), ((0, 0), (0, LANES - reps * FOX_H))).astype(BF16)
    bf = jnp.pad(jnp.tile(b_f, reps), (0, LANES - reps * FOX_H)).reshape(1, LANES).astype(F32)
    tri = jnp.asarray(np.tril(np.ones((ts // 2, ts // 2), np.float32)), BF16)
    eq, ek = _fox_placement()
    eqt, ek = jnp.asarray(eq.T, BF16), jnp.asarray(ek, BF16)
    row = pl.BlockSpec((ts, d), lambda b, i: (b * nt + i, 0))
    wide = pl.BlockSpec((ts, FOX_H * LANES), lambda b, i: (b * nt + i, 0))
    tall = pl.BlockSpec((None, FOX_H * LANES, ts), lambda b, i: (b, 0, i))
    return pl.pallas_call(
        _fox_proj_body,
        grid=(batch, nt),
        in_specs=[row, _pick_spec(g, (1, d)), _const_spec((hd, d)), _const_spec((d, hd)),
                  _const_spec((hd, d)), _const_spec((d, LANES)), _const_spec((1, LANES)),
                  _const_spec(tri.shape), _const_spec(eqt.shape), _const_spec(ek.shape)],
        out_specs=[tall, wide, tall],
        out_shape=[jax.ShapeDtypeStruct((batch, FOX_H * LANES, seq), BF16),
                   jax.ShapeDtypeStruct((n, FOX_H * LANES), BF16),
                   jax.ShapeDtypeStruct((batch, FOX_H * LANES, seq), BF16)],
        scratch_shapes=[pltpu.VMEM((8, LANES), F32)],
        compiler_params=pltpu.CompilerParams(
            dimension_semantics=("parallel", "arbitrary"),
            vmem_limit_bytes=48 * MIB),
        name="fox_proj",
    )(x, gains, wqt, wk, wvt, wf, bf, tri, eqt, ek)


def _fox_attn_body(qt_ref, k_ref, vt_ref, o_ref):
    seq = k_ref.shape[0]
    tq = FOX_TQ
    tk = FOX_TK
    per_q = tq // tk
    row = lax.broadcasted_iota(jnp.int32, (LANES, tq), 0)
    chains = [(qi, hh) for qi in reversed(range(seq // tq)) for hh in range(2)]

    def queries_of(chain, kb):
        skip = tq - tk if kb == (chain[0] + 1) * per_q - 1 else 0
        return skip, tq - skip

    def scores(chain, kb):
        qi, hh = chain
        head = slice(hh * LANES, (hh + 1) * LANES)
        skip, nq = queries_of(chain, kb)
        q0 = qi * tq + skip
        st = _dot(k_ref[kb * tk:(kb + 1) * tk, head], qt_ref[head, q0:q0 + nq])
        if kb >= qi * per_q:
            key = lax.broadcasted_iota(jnp.int32, (tk, nq), 0)
            qry = lax.broadcasted_iota(jnp.int32, (tk, nq), 1)
            st = jnp.where(key + (kb * tk - q0) <= qry, st, NEG)
        return st

    def column_max(chain, sts):
        m = None
        for kb, st in enumerate(sts):
            skip, _ = queries_of(chain, kb)
            mb = jnp.max(st, axis=0, keepdims=True)
            if skip:
                mb = jnp.concatenate([m[:, :skip], jnp.maximum(m[:, skip:], mb)], axis=1)
                m = mb
            else:
                m = mb if m is None else jnp.maximum(m, mb)
        return m

    def probabilities(chain, kb, st, m):
        skip, _ = queries_of(chain, kb)
        return jnp.exp2(st - m[:, skip:]).astype(BF16)

    def weighted_values(chain, kb, pt, acc):
        _, hh = chain
        head = slice(hh * LANES, (hh + 1) * LANES)
        skip, _ = queries_of(chain, kb)
        part = _dot(vt_ref[head, kb * tk:(kb + 1) * tk], pt)
        if acc is None:
            return part
        if skip:
            return jnp.concatenate([acc[:, :skip], acc[:, skip:] + part], axis=1)
        return acc + part

    def blocks(stage):
        return (stage[0][0] + 1) * per_q if stage is not None else 0

    done = {}
    scored = None
    exped = None
    for cur in chains + [None, None]:
        n_cur = (cur[0] + 1) * per_q if cur is not None else 0
        sts, pts, acc = [], [], None
        for kb in range(max(n_cur, blocks(scored), blocks(exped))):
            if kb < n_cur:
                sts.append(scores(cur, kb))
            if kb < blocks(scored):
                pts.append(probabilities(scored[0], kb, scored[1][kb], scored[2]))
            if kb < blocks(exped):
                acc = weighted_values(exped[0], kb, exped[1][kb], acc)
        if exped is not None:
            qi, hh = exped[0]
            ones_row = HEAD_DIM if hh == 0 else 0
            done[hh] = acc / acc[ones_row:ones_row + 1, :]
            if hh == 1:
                out = jnp.where(row < HEAD_DIM, done[0], done[1])
                o_ref[qi * tq:(qi + 1) * tq, :] = out.T.astype(BF16)
        exped = (scored[0], pts) if scored is not None else None
        scored = (cur, sts, column_max(cur, sts)) if cur is not None else None


def _fox_attn(qt, k, vt, batch):
    n = k.shape[0]
    seq = n // batch
    pairs = FOX_H // 2
    k = k.reshape(batch, seq, FOX_H * LANES)
    pair_block = pl.BlockSpec((None, seq, 2 * LANES), lambda b, p: (b, 0, p))
    out = pl.pallas_call(
        _fox_attn_body,
        grid=(batch, pairs),
        in_specs=[pl.BlockSpec((None, 2 * LANES, seq), lambda b, p: (b, p, 0)), pair_block,
                  pl.BlockSpec((None, 2 * LANES, seq), lambda b, p: (b, p, 0))],
        out_specs=pl.BlockSpec((None, seq, LANES), lambda b, p: (b, 0, p)),
        out_shape=jax.ShapeDtypeStruct((batch, seq, FOX_H * HEAD_DIM), BF16),
        compiler_params=pltpu.CompilerParams(
            dimension_semantics=("parallel", "parallel"),
            vmem_limit_bytes=48 * MIB),
        name="fox_attn",
    )(qt, k, vt)
    return out.reshape(n, FOX_H * HEAD_DIM)


def kernel(x, norm_g, ffn_w_gate, ffn_w_up, ffn_w_down, ab_w_in, conv_w, conv_b, conv_ln_g,
           conv_ln_b, swa_sinks, ab_w_out, fox_w_in, fox_b_f, fox_w_out):
    batch, seq, d = x.shape
    depth = norm_g.shape[0]
    xf = x.reshape(batch * seq, d)
    stacks = (ffn_w_gate, ffn_w_up, ffn_w_down)
    w_next = tuple(w[0, 0].astype(BF16) for w in stacks)
    n_norms = norm_g.shape[1]
    gains = norm_g.reshape(depth * n_norms, 1, d)
    conv_b, conv_ln_g, conv_ln_b = (p.reshape(p.shape[0], 1, -1) for p in (conv_b, conv_ln_g, conv_ln_b))
    for l in range(depth):
        g = l * n_norms
        xf, w_next = _ffn(xf, gains, g, g + 1, w_next, cast=(stacks, (l, 1)))
        i = l // 2
        mix = None
        if l % 2 == 0:
            xf = _even_mixer(xf, batch, gains, g + 2, g + 3, i, ab_w_in[i], conv_w, conv_b,
                             conv_ln_g, conv_ln_b, swa_sinks[i], ab_w_out[i])
        else:
            qt, k, vt = _fox_proj(xf, batch, gains, g + 2, fox_w_in[i], fox_b_f[i])
            mix = (_fox_attn(qt, k, vt, batch), fox_w_out[i], g + 3)
        cast = (stacks, (l + 1, 0)) if l + 1 < depth else None
        xf, w_next = _ffn(xf, gains, g + 4, g + 5, w_next, mix, cast)
    return xf.reshape(batch, seq, d)
```

```python
import functools
import math

import jax
import jax.numpy as jnp
import numpy as np
from jax import lax
from jax.experimental import pallas as pl
from jax.experimental.pallas import tpu as pltpu

F32 = jnp.float32
BF16 = jnp.bfloat16

LANES = 128
V7X_VMEM_BYTES = 64 * 1024 * 1024
MIB = 1024 * 1024

D_MODEL = 1024
D_FF = 2816
EPS = 1e-6
HEAD_DIM = 64
CONV_CH = 512
CONV_K = 31
SWA_HQ = 8
SWA_HKV = 2
SWA_G = SWA_HQ // SWA_HKV
WINDOW = 128
BLK = 128
FOX_H = 16
NEG = -1e30

FFN_ROWS = 1024
FFN_SUB_ROWS = 512
FFN_CHUNK = 256
MIX_ROWS = 512
CONV_HALO = 32
CONV_PHASES = 4
FOX_ROWS = 512
FOX_TQ = 512
FOX_TK = 256
N_AUG = 3
LOG2E = 1.4426950408889634


def _rms(x, g):
    ms = jnp.mean(x * x, axis=-1, keepdims=True)
    return x * lax.rsqrt(ms + EPS) * g


def _dot(a, b):
    return jnp.dot(a, b, preferred_element_type=F32)


def _dot_nt(a, b):
    return lax.dot_general(a, b, (((1,), (1,)), ((), ())), preferred_element_type=F32)


def _split_bf16(x):
    p1 = x.astype(BF16)
    r1 = x - p1.astype(F32)
    p2 = r1.astype(BF16)
    r2 = r1 - p2.astype(F32)
    return p1, p2, r2.astype(BF16)


def _const_spec(shape):
    n = len(shape)
    return pl.BlockSpec(shape, lambda *_: (0,) * n, pipeline_mode=pl.Buffered(1))


def _pick_spec(index, shape):
    n = len(shape)
    return pl.BlockSpec((None,) + tuple(shape), lambda *_: (index,) + (0,) * n,
                        pipeline_mode=pl.Buffered(1))


def _ffn_body(*refs, mix_in, cast_next):
    if mix_in:
        attn_ref, wmix_ref, gmix_ref, *refs = refs
    if cast_next:
        (x_ref, gpre_ref, gpost_ref, wg_ref, wu_ref, wd_ref, ng32_ref, nu32_ref, nd32_ref,
         o_ref, ng_ref, nu_ref, nd_ref, a_ref) = refs
        casts = [(ng32_ref, ng_ref), (nu32_ref, nu_ref), (nd32_ref, nd_ref)]
    else:
        casts = []
        x_ref, gpre_ref, gpost_ref, wg_ref, wu_ref, wd_ref, o_ref, a_ref = refs
    tf = FFN_CHUNK
    for r0 in range(0, x_ref.shape[0], FFN_SUB_ROWS):
        rows = slice(r0, r0 + FFN_SUB_ROWS)
        x = x_ref[rows, :]
        if mix_in:
            x = x + _rms(_dot(attn_ref[rows, :], wmix_ref[...]), gmix_ref[...])
        h = _rms(x, gpre_ref[...]).astype(BF16)
        for c in range(wg_ref.shape[1] // tf):
            if casts and c % 2 == 1:
                src, dst = casts.pop()
                dst[...] = src[...].astype(BF16)
            cols = slice(c * tf, (c + 1) * tf)
            g = _dot(h, wg_ref[:, cols])
            u = _dot(h, wu_ref[:, cols])
            a_ref[rows, cols] = (g * jax.nn.sigmoid(g) * u).astype(BF16)
        y = _dot(a_ref[rows, :], wd_ref[...])
        o_ref[rows, :] = x + 0.5 * _rms(y, gpost_ref[...])


def _ffn(x, gains, pre, post, weights_bf16, mix=None, cast=None):
    n, d = x.shape
    tm = FFN_ROWS
    steps = n // tm
    row = pl.BlockSpec((tm, d), lambda i: (i, 0))
    args = [x, gains, gains, *weights_bf16]
    in_specs = [row, _pick_spec(pre, (1, d)), _pick_spec(post, (1, d)),
                _const_spec((d, D_FF)), _const_spec((d, D_FF)), _const_spec((D_FF, d))]
    out_specs = [row]
    out_shape = [jax.ShapeDtypeStruct((n, d), F32)]
    weights = 3 * d * D_FF * 2
    tiles = 4 * tm * d * 4 + tm * D_FF * 2 + 4 * tm * FFN_CHUNK * 4 + 2 * tm * d * 4
    if cast is not None:
        stacks, which = cast
        for w in stacks:
            rows, cols = w.shape[2] // steps, w.shape[3]
            args.append(w)
            in_specs.append(pl.BlockSpec((None, None, rows, cols), lambda i: which + (i, 0)))
            out_specs.append(pl.BlockSpec((rows, cols), lambda i: (i, 0)))
            out_shape.append(jax.ShapeDtypeStruct(w.shape[2:], BF16))
            tiles += rows * cols * (4 + 2)
    if mix is not None:
        attn, w_mix, g_mix = mix
        args = [attn, w_mix.astype(BF16), gains] + args
        in_specs = [row, _const_spec((d, d)), _pick_spec(g_mix, (1, d))] + in_specs
        weights += d * d * 2
        tiles += 2 * tm * d * 2 + tm * d * 4
    out = pl.pallas_call(
        functools.partial(_ffn_body, mix_in=mix is not None, cast_next=cast is not None),
        grid=(steps,),
        in_specs=in_specs,
        out_specs=out_specs,
        out_shape=out_shape,
        scratch_shapes=[pltpu.VMEM((tm, D_FF), BF16)],
        compiler_params=pltpu.CompilerParams(
            dimension_semantics=("parallel",),
            vmem_limit_bytes=min(V7X_VMEM_BYTES - 8 * MIB, weights + 2 * tiles)),
        name="ffn_mix" if mix is not None else "ffn",
    )(*args)
    return out[0], tuple(out[1:])


def _even_body(sinks_ref, x_ref, g2_ref, g3_ref, win32_ref, cw_ref, cb_ref, lng_ref, lnb_ref,
               bias_ref, wout32_ref, o_ref, a_buf, k_buf, vt_buf, y_buf, cat_buf, win_ref, wout_ref):
    i = pl.program_id(1)
    ts = x_ref.shape[0]
    n_slabs = CONV_CH // LANES
    chunk = 2 * LANES
    o_q = 2 * CONV_CH
    o_k = o_q + SWA_HQ * HEAD_DIM

    @pl.when(jnp.logical_and(pl.program_id(0) == 0, i == 0))
    def _():
        for cb in range(n_slabs):
            for part, half in enumerate((0, CONV_CH)):
                src = slice(half + cb * LANES, half + (cb + 1) * LANES)
                dst = slice(cb * chunk + part * LANES, cb * chunk + (part + 1) * LANES)
                win_ref[:, dst] = win32_ref[:, src].astype(BF16)
        scale = LOG2E / math.sqrt(HEAD_DIM)
        for j in range(SWA_G):
            heads = [win32_ref[:, o_q + g * HEAD_DIM:o_q + (g + 1) * HEAD_DIM] for g in (j, SWA_G + j)]
            win_ref[:, o_q + j * LANES:o_q + (j + 1) * LANES] = (
                jnp.concatenate(heads, axis=1) * scale).astype(BF16)
            for part, g in enumerate((j, SWA_G + j)):
                dst = CONV_CH + j * LANES + part * HEAD_DIM
                src = CONV_CH + g * HEAD_DIM
                wout_ref[dst:dst + HEAD_DIM, :] = wout32_ref[src:src + HEAD_DIM, :].astype(BF16)
        win_ref[:, o_k:] = win32_ref[:, o_k:].astype(BF16)
        wout_ref[0:CONV_CH, :] = wout32_ref[0:CONV_CH, :].astype(BF16)

    x = x_ref[...]
    h = _rms(x, g2_ref[...]).astype(BF16)

    @pl.when(i == 0)
    def _():
        a_buf[:, 0:CONV_HALO, :] = jnp.zeros((n_slabs, CONV_HALO, LANES), F32)
        k_buf[0:BLK, :] = jnp.zeros((BLK, LANES), BF16)
        vt_buf[:, 0:BLK] = jnp.zeros((LANES, BLK), BF16)

    @pl.when(i > 0)
    def _():
        a_buf[:, 0:CONV_HALO, :] = a_buf[:, ts:ts + CONV_HALO, :]
        k_buf[0:BLK, :] = k_buf[ts:ts + BLK, :]
        vt_buf[:, 0:BLK] = vt_buf[:, ts:ts + BLK]

    def project(c):
        return _dot(h, win_ref[:, c * chunk:(c + 1) * chunk])

    def gate_into_slab(cb, zc):
        a_buf[cb, CONV_HALO:CONV_HALO + ts, :] = zc[:, :LANES] * jax.nn.sigmoid(zc[:, LANES:])

    first_tap = CONV_HALO - (CONV_K - 1)
    n_rows = ts // CONV_PHASES

    def conv_phase(cb, r):
        cols = slice(cb * LANES, (cb + 1) * LANES)
        acc = jnp.broadcast_to(cb_ref[:, cols], (n_rows, LANES))
        for j in range(CONV_K):
            taps = a_buf[cb, pl.ds(first_tap + r + j, n_rows, stride=CONV_PHASES), :]
            acc = acc + cw_ref[j:j + 1, cols] * taps
        y_buf[cb, pl.ds(r, n_rows, stride=CONV_PHASES), :] = acc

    assert n_slabs == ts // BLK == 4
    gate_into_slab(0, project(0))
    zq = []
    for cb in range(n_slabs - 1):
        conv_phase(cb, 0)
        gate_into_slab(cb + 1, project(cb + 1))
        conv_phase(cb, 1)
        conv_phase(cb, 2)
        if cb < 2:
            zq.append(project(n_slabs + cb))
        else:
            zkv = project(n_slabs + 2)
            k_buf[BLK:BLK + ts, :] = zkv[:, :LANES].astype(BF16)
            vt_buf[:, BLK:BLK + ts] = zkv[:, LANES:].T.astype(BF16)
        conv_phase(cb, 3)

    lane = lax.broadcasted_iota(jnp.int32, (BLK, LANES), 1)
    key_is_prev = lax.broadcasted_iota(jnp.int32, (2 * BLK, SWA_G * BLK), 0) < BLK
    units = [(qb, hk) for qb in range(ts // BLK) for hk in range(SWA_HKV)]
    last = n_slabs - 1

    def unit_scores(qb, hk):
        rows = slice(qb * BLK, (qb + 1) * BLK)
        mine = (lane < HEAD_DIM) if hk == 0 else (lane >= HEAD_DIM)
        qs = jnp.concatenate(
            [jnp.where(mine, zq[j // 2][rows, (j % 2) * LANES:(j % 2 + 1) * LANES], 0.0)
             for j in range(SWA_G)], axis=0).astype(BF16)
        st = _dot_nt(k_buf[qb * BLK:qb * BLK + 2 * BLK, :], qs)
        bias = bias_ref[hk]
        if qb == 0:
            bias = jnp.where(jnp.logical_and(i == 0, key_is_prev), NEG, bias)
        return st + bias

    def unit_softmax(hk, st):
        sink = jnp.concatenate(
            [jnp.full((1, BLK), sinks_ref[SWA_G * hk + j], F32) for j in range(SWA_G)], axis=1)
        m = jnp.maximum(jnp.max(st, axis=0, keepdims=True), sink)
        p = jnp.exp2(st - m)
        den = jnp.sum(p, axis=0, keepdims=True) + jnp.exp2(sink - m)
        return p.astype(BF16), den

    def unit_values(qb, p, den):
        return _dot(vt_buf[:, qb * BLK:qb * BLK + 2 * BLK], p) / den

    conv_phase(last, 0)
    sts = [unit_scores(qb, hk) for qb, hk in units]
    conv_phase(last, 1)
    pds = [unit_softmax(hk, st) for (qb, hk), st in zip(units, sts)]
    conv_phase(last, 2)
    ots = [unit_values(qb, p, den) for (qb, hk), (p, den) in zip(units, pds)]
    conv_phase(last, 3)
    row = lax.broadcasted_iota(jnp.int32, (LANES, SWA_G * BLK), 0)
    for qb in range(ts // BLK):
        out = jnp.where(row < HEAD_DIM, ots[SWA_HKV * qb], ots[SWA_HKV * qb + 1]).T
        for j in range(SWA_G):
            cat_buf[qb * BLK:(qb + 1) * BLK, CONV_CH + j * LANES:CONV_CH + (j + 1) * LANES] = (
                out[j * BLK:(j + 1) * BLK].astype(BF16))

    ys = [y_buf[cb] for cb in range(n_slabs)]
    mu = jnp.sum(sum(ys), axis=-1, keepdims=True) * (1.0 / CONV_CH)
    ycs = [y - mu for y in ys]
    var = jnp.sum(sum(yc * yc for yc in ycs), axis=-1, keepdims=True) * (1.0 / CONV_CH)
    inv = lax.rsqrt(var + EPS)
    for cb in range(n_slabs):
        cols = slice(cb * LANES, (cb + 1) * LANES)
        yn = ycs[cb] * inv * lng_ref[:, cols] + lnb_ref[:, cols]
        cat_buf[:, cols] = (yn * jax.nn.sigmoid(yn)).astype(BF16)

    y2 = _dot(cat_buf[...], wout_ref[...])
    o_ref[...] = x + _rms(y2, g3_ref[...])


def _swa_bias_table():
    t_loc = np.arange(BLK)[None, :]
    s_loc = np.arange(2 * BLK)[:, None]
    dist = t_loc + BLK - s_loc
    valid = (dist >= 0) & (dist < WINDOW)
    slopes = [2.0 ** (-8.0 * (g + 1) / SWA_HQ) for g in range(SWA_HQ)]
    per_head = [np.where(valid, -np.float32(sl * LOG2E) * dist.astype(np.float32), np.float32(NEG))
                for sl in slopes]
    return np.stack([np.concatenate(per_head[SWA_G * hk:SWA_G * (hk + 1)], axis=1)
                     for hk in range(SWA_HKV)]).astype(np.float32)


def _even_mixer(x, batch, gains, g2, g3, layer, w_in, conv_w, conv_b, ln_g, ln_b, sinks, w_out):
    n, d = x.shape
    seq = n // batch
    ts = MIX_ROWS
    nt = seq // ts
    ab_in = w_in.shape[2]
    bias = _swa_bias_table()
    row = pl.BlockSpec((ts, d), lambda b, i, *_: (b * nt + i, 0))

    def const(shape):
        k = len(shape)
        return pl.BlockSpec(shape, lambda b, i, *_: (0,) * k, pipeline_mode=pl.Buffered(1))

    grid_spec = pltpu.PrefetchScalarGridSpec(
        num_scalar_prefetch=1,
        grid=(batch, nt),
        in_specs=[row, _pick_spec(g2, (1, d)), _pick_spec(g3, (1, d)), _pick_spec(layer, (d, ab_in)),
                  _pick_spec(layer, (CONV_K, CONV_CH)), _pick_spec(layer, (1, CONV_CH)),
                  _pick_spec(layer, (1, CONV_CH)), _pick_spec(layer, (1, CONV_CH)),
                  const(bias.shape), _pick_spec(layer, (d, d))],
        out_specs=row,
        scratch_shapes=[pltpu.VMEM((CONV_CH // LANES, CONV_HALO + ts, LANES), F32),
                        pltpu.VMEM((BLK + ts, LANES), BF16),
                        pltpu.VMEM((LANES, BLK + ts), BF16),
                        pltpu.VMEM((CONV_CH // LANES, ts, LANES), F32),
                        pltpu.VMEM((ts, d), BF16),
                        pltpu.VMEM((d, ab_in), BF16),
                        pltpu.VMEM((d, d), BF16)])
    return pl.pallas_call(
        _even_body,
        grid_spec=grid_spec,
        out_shape=jax.ShapeDtypeStruct((n, d), F32),
        compiler_params=pltpu.CompilerParams(
            dimension_semantics=("arbitrary", "arbitrary"),
            vmem_limit_bytes=52 * MIB),
        name="even_mixer",
    )(sinks * LOG2E, x, gains, gains, w_in, conv_w, conv_b, ln_g, ln_b, bias, w_out)


def _fox_proj_body(x_ref, g_ref, wq_ref, wk_ref, wv_ref, wf_ref, bf_ref, tri_ref, eqt_ref, ek_ref,
                   qt_ref, k_ref, vt_ref, carry_ref, wqt_ref, wvt_ref):
    i = pl.program_id(1)
    ts = x_ref.shape[0]

    @pl.when(i == 0)
    def _():
        carry_ref[...] = jnp.zeros(carry_ref.shape, F32)

    @pl.when(jnp.logical_and(pl.program_id(0) == 0, i == 0))
    def _():
        for c0 in range(0, wq_ref.shape[1], 2 * LANES):
            cols = slice(c0, c0 + 2 * LANES)
            wqt_ref[cols, :] = (wq_ref[:, cols] * (LOG2E / math.sqrt(HEAD_DIM))).T.astype(BF16)
            wvt_ref[cols, :] = wv_ref[:, cols].T.astype(BF16)

    h = _rms(x_ref[...], g_ref[...]).astype(BF16)

    t = _dot(h, wf_ref[...]) + bf_ref[...]
    zqt = _dot_nt(wqt_ref[...], h)
    logf = jnp.minimum(t, 0.0) - jnp.log1p(jnp.exp(-jnp.abs(t)))
    tri = tri_ref[...]
    half = ts // 2
    total = carry_ref[0:1, :]
    halves = []
    for r0 in (0, half):
        ch = total
        for piece in _split_bf16(logf[r0:r0 + half]):
            ch = ch + _dot(tri, piece)
        total = ch[half - 1:half, :]
        halves.append(ch)
    c = jnp.concatenate(halves, axis=0)
    carry_ref[0:1, :] = total
    zk = _dot(h, wk_ref[...])

    c1, c2, c3 = _split_bf16(c * LOG2E)
    lane = lax.broadcasted_iota(jnp.int32, (ts, LANES), 1)
    cp = jnp.where(lane < FOX_H, c1,
                   jnp.where(lane < 2 * FOX_H, c2,
                             jnp.where(lane < 3 * FOX_H, c3,
                                       jnp.where(lane < 4 * FOX_H, 1.0, 0.0).astype(BF16))))
    aqt = _dot_nt(eqt_ref[...], cp)
    ak = _dot(cp, ek_ref[...])
    zvt = _dot_nt(wvt_ref[...], h)
    low = lane < HEAD_DIM
    for p in range(FOX_H // 2):
        pair = slice(p * LANES, (p + 1) * LANES)
        even = slice(2 * p * LANES, (2 * p + 1) * LANES)
        odd = slice((2 * p + 1) * LANES, (2 * p + 2) * LANES)
        k_ref[:, even] = jnp.where(low, zk[:, pair], ak[:, pair]).astype(BF16)
        k_ref[:, odd] = jnp.where(low, ak[:, pair], zk[:, pair]).astype(BF16)
    row = lax.broadcasted_iota(jnp.int32, (LANES, ts), 0)
    top = row < HEAD_DIM
    one_mid = (row == HEAD_DIM).astype(F32)
    one_top = (row == 0).astype(F32)
    for p in range(FOX_H // 2):
        pair = slice(p * LANES, (p + 1) * LANES)
        even = slice(2 * p * LANES, (2 * p + 1) * LANES)
        odd = slice((2 * p + 1) * LANES, (2 * p + 2) * LANES)
        vt_ref[even, :] = jnp.where(top, zvt[pair, :], one_mid).astype(BF16)
        vt_ref[odd, :] = jnp.where(top, one_top, zvt[pair, :]).astype(BF16)
        qt_ref[even, :] = jnp.where(top, zqt[pair, :], aqt[pair, :]).astype(BF16)
        qt_ref[odd, :] = jnp.where(top, aqt[pair, :], zqt[pair, :]).astype(BF16)


def _fox_placement():
    eq = np.zeros((LANES, FOX_H // 2 * LANES), np.float32)
    ek = np.zeros((LANES, FOX_H // 2 * LANES), np.float32)
    for h in range(FOX_H):
        base = (h // 2) * LANES + (HEAD_DIM if h % 2 == 0 else 0)
        for a in range(N_AUG):
            eq[a * FOX_H + h, base + a] = 1.0
            eq[N_AUG * FOX_H + h, base + N_AUG + a] = 1.0
            ek[N_AUG * FOX_H + h, base + a] = 1.0
            ek[a * FOX_H + h, base + N_AUG + a] = -1.0
    return eq, ek


def _fox_proj(x, batch, gains, g, layer, w_in_stack, b_f):
    n, d = x.shape
    w_in = w_in_stack[layer]
    seq = n // batch
    ts = FOX_ROWS
    nt = seq // ts
    hd = FOX_H * HEAD_DIM
    wk = w_in[:, hd:2 * hd].astype(BF16)

    def columns(block):
        return pl.BlockSpec((None, d, hd), lambda *_: (layer, 0, block), pipeline_mode=pl.Buffered(1))
    reps = N_AUG
    wf = jnp.pad(jnp.tile(w_in[:, 3 * hd:], (1, reps)), ((0, 0), (0, LANES - reps * FOX_H))).astype(BF16)
    bf = jnp.pad(jnp.tile(b_f, reps), (0, LANES - reps * FOX_H)).reshape(1, LANES).astype(F32)
    tri = jnp.asarray(np.tril(np.ones((ts // 2, ts // 2), np.float32)), BF16)
    eq, ek = _fox_placement()
    eqt, ek = jnp.asarray(eq.T, BF16), jnp.asarray(ek, BF16)
    row = pl.BlockSpec((ts, d), lambda b, i: (b * nt + i, 0))
    wide = pl.BlockSpec((ts, FOX_H * LANES), lambda b, i: (b * nt + i, 0))
    tall = pl.BlockSpec((None, FOX_H * LANES, ts), lambda b, i: (b, 0, i))
    return pl.pallas_call(
        _fox_proj_body,
        grid=(batch, nt),
        in_specs=[row, _pick_spec(g, (1, d)), columns(0), _const_spec((d, hd)),
                  columns(2), _const_spec((d, LANES)), _const_spec((1, LANES)),
                  _const_spec(tri.shape), _const_spec(eqt.shape), _const_spec(ek.shape)],
        out_specs=[tall, wide, tall],
        out_shape=[jax.ShapeDtypeStruct((batch, FOX_H * LANES, seq), BF16),
                   jax.ShapeDtypeStruct((n, FOX_H * LANES), BF16),
                   jax.ShapeDtypeStruct((batch, FOX_H * LANES, seq), BF16)],
        scratch_shapes=[pltpu.VMEM((8, LANES), F32), pltpu.VMEM((hd, d), BF16), pltpu.VMEM((hd, d), BF16)],
        compiler_params=pltpu.CompilerParams(
            dimension_semantics=("arbitrary", "arbitrary"),
            vmem_limit_bytes=52 * MIB),
        name="fox_proj",
    )(x, gains, w_in_stack, wk, w_in_stack, wf, bf, tri, eqt, ek)


def _fox_attn_body(qt_ref, k_ref, vt_ref, o_ref):
    seq = k_ref.shape[0]
    tq = FOX_TQ
    tk = FOX_TK
    per_q = tq // tk
    row = lax.broadcasted_iota(jnp.int32, (LANES, tq), 0)
    chains = [(qi, hh) for qi in reversed(range(seq // tq)) for hh in range(2)]

    def queries_of(chain, kb):
        skip = tq - tk if kb == (chain[0] + 1) * per_q - 1 else 0
        return skip, tq - skip

    def scores(chain, kb):
        qi, hh = chain
        head = slice(hh * LANES, (hh + 1) * LANES)
        skip, nq = queries_of(chain, kb)
        q0 = qi * tq + skip
        st = _dot(k_ref[kb * tk:(kb + 1) * tk, head], qt_ref[head, q0:q0 + nq])
        if kb >= qi * per_q:
            key = lax.broadcasted_iota(jnp.int32, (tk, nq), 0)
            qry = lax.broadcasted_iota(jnp.int32, (tk, nq), 1)
            st = jnp.where(key + (kb * tk - q0) <= qry, st, NEG)
        return st

    def column_max(chain, sts):
        m = None
        for kb, st in enumerate(sts):
            skip, _ = queries_of(chain, kb)
            mb = jnp.max(st, axis=0, keepdims=True)
            if skip:
                mb = jnp.concatenate([m[:, :skip], jnp.maximum(m[:, skip:], mb)], axis=1)
                m = mb
            else:
                m = mb if m is None else jnp.maximum(m, mb)
        return m

    def probabilities(chain, kb, st, m):
        skip, _ = queries_of(chain, kb)
        return jnp.exp2(st - m[:, skip:]).astype(BF16)

    def weighted_values(chain, kb, pt, acc):
        _, hh = chain
        head = slice(hh * LANES, (hh + 1) * LANES)
        skip, _ = queries_of(chain, kb)
        part = _dot(vt_ref[head, kb * tk:(kb + 1) * tk], pt)
        if acc is None:
            return part
        if skip:
            return jnp.concatenate([acc[:, :skip], acc[:, skip:] + part], axis=1)
        return acc + part

    def blocks(stage):
        return (stage[0][0] + 1) * per_q if stage is not None else 0

    done = {}
    scored = None
    exped = None
    for cur in chains + [None, None]:
        n_cur = (cur[0] + 1) * per_q if cur is not None else 0
        sts, pts, acc = [], [], None
        for kb in range(max(n_cur, blocks(scored), blocks(exped))):
            if kb < n_cur:
                sts.append(scores(cur, kb))
            if kb < blocks(scored):
                pts.append(probabilities(scored[0], kb, scored[1][kb], scored[2]))
            if kb < blocks(exped):
                acc = weighted_values(exped[0], kb, exped[1][kb], acc)
        if exped is not None:
            qi, hh = exped[0]
            ones_row = HEAD_DIM if hh == 0 else 0
            done[hh] = acc / acc[ones_row:ones_row + 1, :]
            if hh == 1:
                out = jnp.where(row < HEAD_DIM, done[0], done[1])
                o_ref[qi * tq:(qi + 1) * tq, :] = out.T.astype(BF16)
        exped = (scored[0], pts) if scored is not None else None
        scored = (cur, sts, column_max(cur, sts)) if cur is not None else None


def _fox_attn(qt, k, vt, batch):
    n = k.shape[0]
    seq = n // batch
    pairs = FOX_H // 2
    k = k.reshape(batch, seq, FOX_H * LANES)
    pair_block = pl.BlockSpec((None, seq, 2 * LANES), lambda b, p: (b, 0, p))
    out = pl.pallas_call(
        _fox_attn_body,
        grid=(batch, pairs),
        in_specs=[pl.BlockSpec((None, 2 * LANES, seq), lambda b, p: (b, p, 0)), pair_block,
                  pl.BlockSpec((None, 2 * LANES, seq), lambda b, p: (b, p, 0))],
        out_specs=pl.BlockSpec((None, seq, LANES), lambda b, p: (b, 0, p)),
        out_shape=jax.ShapeDtypeStruct((batch, seq, FOX_H * HEAD_DIM), BF16),
        compiler_params=pltpu.CompilerParams(
            dimension_semantics=("parallel", "parallel"),
            vmem_limit_bytes=48 * MIB),
        name="fox_attn",
    )(qt, k, vt)
    return out.reshape(n, FOX_H * HEAD_DIM)


def kernel(x, norm_g, ffn_w_gate, ffn_w_up, ffn_w_down, ab_w_in, conv_w, conv_b, conv_ln_g,
           conv_ln_b, swa_sinks, ab_w_out, fox_w_in, fox_b_f, fox_w_out):
    batch, seq, d = x.shape
    depth = norm_g.shape[0]
    xf = x.reshape(batch * seq, d)
    stacks = (ffn_w_gate, ffn_w_up, ffn_w_down)
    w_next = tuple(w[0, 0].astype(BF16) for w in stacks)
    n_norms = norm_g.shape[1]
    gains = norm_g.reshape(depth * n_norms, 1, d)
    conv_b, conv_ln_g, conv_ln_b = (p.reshape(p.shape[0], 1, -1) for p in (conv_b, conv_ln_g, conv_ln_b))
    for l in range(depth):
        g = l * n_norms
        xf, w_next = _ffn(xf, gains, g, g + 1, w_next, cast=(stacks, (l, 1)))
        i = l // 2
        mix = None
        if l % 2 == 0:
            xf = _even_mixer(xf, batch, gains, g + 2, g + 3, i, ab_w_in, conv_w, conv_b,
                             conv_ln_g, conv_ln_b, swa_sinks[i], ab_w_out)
        else:
            qt, k, vt = _fox_proj(xf, batch, gains, g + 2, i, fox_w_in, fox_b_f[i])
            mix = (_fox_attn(qt, k, vt, batch), fox_w_out[i], g + 3)
        cast = (stacks, (l + 1, 0)) if l + 1 < depth else None
        xf, w_next = _ffn(xf, gains, g + 4, g + 5, w_next, mix, cast)
    return xf.reshape(batch, seq, d)
```

```python
import functools
import math

import jax
import jax.numpy as jnp
import numpy as np
from jax import lax
from jax.experimental import pallas as pl
from jax.experimental.pallas import tpu as pltpu

F32 = jnp.float32
BF16 = jnp.bfloat16

LANES = 128
V7X_VMEM_BYTES = 64 * 1024 * 1024
MIB = 1024 * 1024

D_MODEL = 1024
D_FF = 2816
EPS = 1e-6
HEAD_DIM = 64
CONV_CH = 512
CONV_K = 31
SWA_HQ = 8
SWA_HKV = 2
SWA_G = SWA_HQ // SWA_HKV
WINDOW = 128
BLK = 128
FOX_H = 16
NEG = -1e30

FFN_ROWS = 1024
FFN_SUB_ROWS = 512
FFN_CHUNK = 256
MIX_ROWS = 512
CONV_HALO = 32
CONV_PHASES = 4
FOX_ROWS = 512
FOX_TQ = 512
FOX_TK = 256
N_AUG = 3
LOG2E = 1.4426950408889634


def _rms(x, g):
    ms = jnp.mean(x * x, axis=-1, keepdims=True)
    return x * lax.rsqrt(ms + EPS) * g


def _dot(a, b):
    return jnp.dot(a, b, preferred_element_type=F32)


def _dot_nt(a, b):
    return lax.dot_general(a, b, (((1,), (1,)), ((), ())), preferred_element_type=F32)


def _split_bf16(x):
    p1 = x.astype(BF16)
    r1 = x - p1.astype(F32)
    p2 = r1.astype(BF16)
    r2 = r1 - p2.astype(F32)
    return p1, p2, r2.astype(BF16)


def _const_spec(shape):
    n = len(shape)
    return pl.BlockSpec(shape, lambda *_: (0,) * n, pipeline_mode=pl.Buffered(1))


def _pick_spec(index, shape):
    n = len(shape)
    return pl.BlockSpec((None,) + tuple(shape), lambda *_: (index,) + (0,) * n,
                        pipeline_mode=pl.Buffered(1))


def _ffn_body(*refs, mix_in, cast_next):
    if mix_in:
        attn_ref, wmix_ref, gmix_ref, *refs = refs
    if cast_next:
        (x_ref, gpre_ref, gpost_ref, wg_ref, wu_ref, wd_ref, ng32_ref, nu32_ref, nd32_ref,
         o_ref, ng_ref, nu_ref, nd_ref, a_ref) = refs
        casts = [(ng32_ref, ng_ref), (nu32_ref, nu_ref), (nd32_ref, nd_ref)]
    else:
        casts = []
        x_ref, gpre_ref, gpost_ref, wg_ref, wu_ref, wd_ref, o_ref, a_ref = refs
    tf = FFN_CHUNK
    for r0 in range(0, x_ref.shape[0], FFN_SUB_ROWS):
        rows = slice(r0, r0 + FFN_SUB_ROWS)
        x = x_ref[rows, :]
        if mix_in:
            x = x + _rms(_dot(attn_ref[rows, :], wmix_ref[...]), gmix_ref[...])
        h = _rms(x, gpre_ref[...]).astype(BF16)
        for c in range(wg_ref.shape[1] // tf):
            if casts and c % 2 == 1:
                src, dst = casts.pop()
                dst[...] = src[...].astype(BF16)
            cols = slice(c * tf, (c + 1) * tf)
            g = _dot(h, wg_ref[:, cols])
            u = _dot(h, wu_ref[:, cols])
            a_ref[rows, cols] = (g * jax.nn.sigmoid(g) * u).astype(BF16)
        y = _dot(a_ref[rows, :], wd_ref[...])
        o_ref[rows, :] = x + 0.5 * _rms(y, gpost_ref[...])


def _ffn(x, gains, pre, post, weights_bf16, mix=None, cast=None):
    n, d = x.shape
    tm = FFN_ROWS
    steps = n // tm
    row = pl.BlockSpec((tm, d), lambda i: (i, 0))
    args = [x, gains, gains, *weights_bf16]
    in_specs = [row, _pick_spec(pre, (1, d)), _pick_spec(post, (1, d)),
                _const_spec((d, D_FF)), _const_spec((d, D_FF)), _const_spec((D_FF, d))]
    out_specs = [row]
    out_shape = [jax.ShapeDtypeStruct((n, d), F32)]
    weights = 3 * d * D_FF * 2
    tiles = 4 * tm * d * 4 + tm * D_FF * 2 + 4 * tm * FFN_CHUNK * 4 + 2 * tm * d * 4
    if cast is not None:
        stacks, which = cast
        for w in stacks:
            rows, cols = w.shape[2] // steps, w.shape[3]
            args.append(w)
            in_specs.append(pl.BlockSpec((None, None, rows, cols), lambda i: which + (i, 0)))
            out_specs.append(pl.BlockSpec((rows, cols), lambda i: (i, 0)))
            out_shape.append(jax.ShapeDtypeStruct(w.shape[2:], BF16))
            tiles += rows * cols * (4 + 2)
    if mix is not None:
        attn, w_mix, g_mix = mix
        args = [attn, w_mix.astype(BF16), gains] + args
        in_specs = [row, _const_spec((d, d)), _pick_spec(g_mix, (1, d))] + in_specs
        weights += d * d * 2
        tiles += 2 * tm * d * 2 + tm * d * 4
    out = pl.pallas_call(
        functools.partial(_ffn_body, mix_in=mix is not None, cast_next=cast is not None),
        grid=(steps,),
        in_specs=in_specs,
        out_specs=out_specs,
        out_shape=out_shape,
        scratch_shapes=[pltpu.VMEM((tm, D_FF), BF16)],
        compiler_params=pltpu.CompilerParams(
            dimension_semantics=("parallel",),
            vmem_limit_bytes=min(V7X_VMEM_BYTES - 8 * MIB, weights + 2 * tiles)),
        name="ffn_mix" if mix is not None else "ffn",
    )(*args)
    return out[0], tuple(out[1:])


def _even_body(sinks_ref, x_ref, g2_ref, g3_ref, win32_ref, cw_ref, cb_ref, lng_ref, lnb_ref,
               bias_ref, wout32_ref, o_ref, a_buf, k_buf, vt_buf, y_buf, cat_buf, win_ref, wout_ref):
    i = pl.program_id(1)
    ts = x_ref.shape[0]
    n_slabs = CONV_CH // LANES
    chunk = 2 * LANES
    o_q = 2 * CONV_CH
    o_k = o_q + SWA_HQ * HEAD_DIM

    @pl.when(jnp.logical_and(pl.program_id(0) == 0, i == 0))
    def _():
        for cb in range(n_slabs):
            for part, half in enumerate((0, CONV_CH)):
                src = slice(half + cb * LANES, half + (cb + 1) * LANES)
                dst = slice(cb * chunk + part * LANES, cb * chunk + (part + 1) * LANES)
                win_ref[:, dst] = win32_ref[:, src].astype(BF16)
        scale = LOG2E / math.sqrt(HEAD_DIM)
        for j in range(SWA_G):
            heads = [win32_ref[:, o_q + g * HEAD_DIM:o_q + (g + 1) * HEAD_DIM] for g in (j, SWA_G + j)]
            win_ref[:, o_q + j * LANES:o_q + (j + 1) * LANES] = (
                jnp.concatenate(heads, axis=1) * scale).astype(BF16)
            for part, g in enumerate((j, SWA_G + j)):
                dst = CONV_CH + j * LANES + part * HEAD_DIM
                src = CONV_CH + g * HEAD_DIM
                wout_ref[dst:dst + HEAD_DIM, :] = wout32_ref[src:src + HEAD_DIM, :].astype(BF16)
        win_ref[:, o_k:] = win32_ref[:, o_k:].astype(BF16)
        wout_ref[0:CONV_CH, :] = wout32_ref[0:CONV_CH, :].astype(BF16)

    x = x_ref[...]
    h = _rms(x, g2_ref[...]).astype(BF16)

    @pl.when(i == 0)
    def _():
        a_buf[:, 0:CONV_HALO, :] = jnp.zeros((n_slabs, CONV_HALO, LANES), F32)
        k_buf[0:BLK, :] = jnp.zeros((BLK, LANES), BF16)
        vt_buf[:, 0:BLK] = jnp.zeros((LANES, BLK), BF16)

    @pl.when(i > 0)
    def _():
        a_buf[:, 0:CONV_HALO, :] = a_buf[:, ts:ts + CONV_HALO, :]
        k_buf[0:BLK, :] = k_buf[ts:ts + BLK, :]
        vt_buf[:, 0:BLK] = vt_buf[:, ts:ts + BLK]

    def project(c):
        return _dot(h, win_ref[:, c * chunk:(c + 1) * chunk])

    def gate_into_slab(cb, zc):
        a_buf[cb, CONV_HALO:CONV_HALO + ts, :] = zc[:, :LANES] * jax.nn.sigmoid(zc[:, LANES:])

    first_tap = CONV_HALO - (CONV_K - 1)
    n_rows = ts // CONV_PHASES

    def conv_phase(cb, r):
        cols = slice(cb * LANES, (cb + 1) * LANES)
        acc = jnp.broadcast_to(cb_ref[:, cols], (n_rows, LANES))
        for j in range(CONV_K):
            taps = a_buf[cb, pl.ds(first_tap + r + j, n_rows, stride=CONV_PHASES), :]
            acc = acc + cw_ref[j:j + 1, cols] * taps
        y_buf[cb, pl.ds(r, n_rows, stride=CONV_PHASES), :] = acc

    assert n_slabs == ts // BLK == 4
    gate_into_slab(0, project(0))
    zq = []
    for cb in range(n_slabs - 1):
        conv_phase(cb, 0)
        gate_into_slab(cb + 1, project(cb + 1))
        conv_phase(cb, 1)
        conv_phase(cb, 2)
        if cb < 2:
            zq.append(project(n_slabs + cb))
        else:
            zkv = project(n_slabs + 2)
            k_buf[BLK:BLK + ts, :] = zkv[:, :LANES].astype(BF16)
            vt_buf[:, BLK:BLK + ts] = zkv[:, LANES:].T.astype(BF16)
        conv_phase(cb, 3)

    lane = lax.broadcasted_iota(jnp.int32, (BLK, LANES), 1)
    key_is_prev = lax.broadcasted_iota(jnp.int32, (2 * BLK, SWA_G * BLK), 0) < BLK
    units = [(qb, hk) for qb in range(ts // BLK) for hk in range(SWA_HKV)]
    last = n_slabs - 1

    def unit_scores(qb, hk):
        rows = slice(qb * BLK, (qb + 1) * BLK)
        mine = (lane < HEAD_DIM) if hk == 0 else (lane >= HEAD_DIM)
        qs = jnp.concatenate(
            [jnp.where(mine, zq[j // 2][rows, (j % 2) * LANES:(j % 2 + 1) * LANES], 0.0)
             for j in range(SWA_G)], axis=0).astype(BF16)
        st = _dot_nt(k_buf[qb * BLK:qb * BLK + 2 * BLK, :], qs)
        bias = bias_ref[hk]
        if qb == 0:
            bias = jnp.where(jnp.logical_and(i == 0, key_is_prev), NEG, bias)
        return st + bias

    def unit_softmax(hk, st):
        sink = jnp.concatenate(
            [jnp.full((1, BLK), sinks_ref[SWA_G * hk + j], F32) for j in range(SWA_G)], axis=1)
        m = jnp.maximum(jnp.max(st, axis=0, keepdims=True), sink)
        p = jnp.exp2(st - m)
        den = jnp.sum(p, axis=0, keepdims=True) + jnp.exp2(sink - m)
        return p.astype(BF16), den

    def unit_values(qb, p, den):
        return _dot(vt_buf[:, qb * BLK:qb * BLK + 2 * BLK], p) / den

    conv_phase(last, 0)
    sts = [unit_scores(qb, hk) for qb, hk in units]
    conv_phase(last, 1)
    pds = [unit_softmax(hk, st) for (qb, hk), st in zip(units, sts)]
    conv_phase(last, 2)
    ots = [unit_values(qb, p, den) for (qb, hk), (p, den) in zip(units, pds)]
    conv_phase(last, 3)
    row = lax.broadcasted_iota(jnp.int32, (LANES, SWA_G * BLK), 0)
    for qb in range(ts // BLK):
        out = jnp.where(row < HEAD_DIM, ots[SWA_HKV * qb], ots[SWA_HKV * qb + 1]).T
        for j in range(SWA_G):
            cat_buf[qb * BLK:(qb + 1) * BLK, CONV_CH + j * LANES:CONV_CH + (j + 1) * LANES] = (
                out[j * BLK:(j + 1) * BLK].astype(BF16))

    ys = [y_buf[cb] for cb in range(n_slabs)]
    mu = jnp.sum(sum(ys), axis=-1, keepdims=True) * (1.0 / CONV_CH)
    ycs = [y - mu for y in ys]
    var = jnp.sum(sum(yc * yc for yc in ycs), axis=-1, keepdims=True) * (1.0 / CONV_CH)
    inv = lax.rsqrt(var + EPS)
    for cb in range(n_slabs):
        cols = slice(cb * LANES, (cb + 1) * LANES)
        yn = ycs[cb] * inv * lng_ref[:, cols] + lnb_ref[:, cols]
        cat_buf[:, cols] = (yn * jax.nn.sigmoid(yn)).astype(BF16)

    y2 = _dot(cat_buf[...], wout_ref[...])
    o_ref[...] = x + _rms(y2, g3_ref[...])


def _swa_bias_table():
    t_loc = np.arange(BLK)[None, :]
    s_loc = np.arange(2 * BLK)[:, None]
    dist = t_loc + BLK - s_loc
    valid = (dist >= 0) & (dist < WINDOW)
    slopes = [2.0 ** (-8.0 * (g + 1) / SWA_HQ) for g in range(SWA_HQ)]
    per_head = [np.where(valid, -np.float32(sl * LOG2E) * dist.astype(np.float32), np.float32(NEG))
                for sl in slopes]
    return np.stack([np.concatenate(per_head[SWA_G * hk:SWA_G * (hk + 1)], axis=1)
                     for hk in range(SWA_HKV)]).astype(np.float32)


def _even_mixer(x, batch, gains, g2, g3, layer, w_in, conv_w, conv_b, ln_g, ln_b, sinks, w_out):
    n, d = x.shape
    seq = n // batch
    ts = MIX_ROWS
    nt = seq // ts
    ab_in = w_in.shape[2]
    bias = _swa_bias_table()
    row = pl.BlockSpec((ts, d), lambda b, i, *_: (b * nt + i, 0))

    def const(shape):
        k = len(shape)
        return pl.BlockSpec(shape, lambda b, i, *_: (0,) * k, pipeline_mode=pl.Buffered(1))

    grid_spec = pltpu.PrefetchScalarGridSpec(
        num_scalar_prefetch=1,
        grid=(batch, nt),
        in_specs=[row, _pick_spec(g2, (1, d)), _pick_spec(g3, (1, d)), _pick_spec(layer, (d, ab_in)),
                  _pick_spec(layer, (CONV_K, CONV_CH)), _pick_spec(layer, (1, CONV_CH)),
                  _pick_spec(layer, (1, CONV_CH)), _pick_spec(layer, (1, CONV_CH)),
                  const(bias.shape), _pick_spec(layer, (d, d))],
        out_specs=row,
        scratch_shapes=[pltpu.VMEM((CONV_CH // LANES, CONV_HALO + ts, LANES), F32),
                        pltpu.VMEM((BLK + ts, LANES), BF16),
                        pltpu.VMEM((LANES, BLK + ts), BF16),
                        pltpu.VMEM((CONV_CH // LANES, ts, LANES), F32),
                        pltpu.VMEM((ts, d), BF16),
                        pltpu.VMEM((d, ab_in), BF16),
                        pltpu.VMEM((d, d), BF16)])
    return pl.pallas_call(
        _even_body,
        grid_spec=grid_spec,
        out_shape=jax.ShapeDtypeStruct((n, d), F32),
        compiler_params=pltpu.CompilerParams(
            dimension_semantics=("arbitrary", "arbitrary"),
            vmem_limit_bytes=52 * MIB),
        name="even_mixer",
    )(sinks * LOG2E, x, gains, gains, w_in, conv_w, conv_b, ln_g, ln_b, bias, w_out)


def _fox_proj_body(x_ref, g_ref, wq_ref, wk_ref, wv_ref, wf_ref, bf_ref, tri_ref, eqt_ref, ek_ref,
                   qt_ref, k_ref, vt_ref, carry_ref, wqt_ref, wvt_ref):
    i = pl.program_id(1)
    ts = x_ref.shape[0]

    @pl.when(i == 0)
    def _():
        carry_ref[...] = jnp.zeros(carry_ref.shape, F32)

    @pl.when(jnp.logical_and(pl.program_id(0) == 0, i == 0))
    def _():
        for c0 in range(0, wq_ref.shape[1], 2 * LANES):
            cols = slice(c0, c0 + 2 * LANES)
            wqt_ref[cols, :] = (wq_ref[:, cols] * (LOG2E / math.sqrt(HEAD_DIM))).T.astype(BF16)
            wvt_ref[cols, :] = wv_ref[:, cols].T.astype(BF16)

    h = _rms(x_ref[...], g_ref[...]).astype(BF16)

    t = _dot(h, wf_ref[...]) + bf_ref[...]
    zqt = _dot_nt(wqt_ref[...], h)
    logf = jnp.minimum(t, 0.0) - jnp.log1p(jnp.exp(-jnp.abs(t)))
    tri = tri_ref[...]
    half = ts // 2
    total = carry_ref[0:1, :]
    halves = []
    for r0 in (0, half):
        ch = total
        for piece in _split_bf16(logf[r0:r0 + half]):
            ch = ch + _dot(tri, piece)
        total = ch[half - 1:half, :]
        halves.append(ch)
    c = jnp.concatenate(halves, axis=0)
    carry_ref[0:1, :] = total
    zk = _dot(h, wk_ref[...])

    c1, c2, c3 = _split_bf16(c * LOG2E)
    lane = lax.broadcasted_iota(jnp.int32, (ts, LANES), 1)
    cp = jnp.where(lane < FOX_H, c1,
                   jnp.where(lane < 2 * FOX_H, c2,
                             jnp.where(lane < 3 * FOX_H, c3,
                                       jnp.where(lane < 4 * FOX_H, 1.0, 0.0).astype(BF16))))
    aqt = _dot_nt(eqt_ref[...], cp)
    ak = _dot(cp, ek_ref[...])
    zvt = _dot_nt(wvt_ref[...], h)
    low = lane < HEAD_DIM
    for p in range(FOX_H // 2):
        pair = slice(p * LANES, (p + 1) * LANES)
        even = slice(2 * p * LANES, (2 * p + 1) * LANES)
        odd = slice((2 * p + 1) * LANES, (2 * p + 2) * LANES)
        k_ref[:, even] = jnp.where(low, zk[:, pair], ak[:, pair]).astype(BF16)
        k_ref[:, odd] = jnp.where(low, ak[:, pair], zk[:, pair]).astype(BF16)
    row = lax.broadcasted_iota(jnp.int32, (LANES, ts), 0)
    top = row < HEAD_DIM
    one_mid = (row == HEAD_DIM).astype(F32)
    one_top = (row == 0).astype(F32)
    for p in range(FOX_H // 2):
        pair = slice(p * LANES, (p + 1) * LANES)
        even = slice(2 * p * LANES, (2 * p + 1) * LANES)
        odd = slice((2 * p + 1) * LANES, (2 * p + 2) * LANES)
        vt_ref[even, :] = jnp.where(top, zvt[pair, :], one_mid).astype(BF16)
        vt_ref[odd, :] = jnp.where(top, one_top, zvt[pair, :]).astype(BF16)
        qt_ref[even, :] = jnp.where(top, zqt[pair, :], aqt[pair, :]).astype(BF16)
        qt_ref[odd, :] = jnp.where(top, aqt[pair, :], zqt[pair, :]).astype(BF16)


def _fox_placement():
    eq = np.zeros((LANES, FOX_H // 2 * LANES), np.float32)
    ek = np.zeros((LANES, FOX_H // 2 * LANES), np.float32)
    for h in range(FOX_H):
        base = (h // 2) * LANES + (HEAD_DIM if h % 2 == 0 else 0)
        for a in range(N_AUG):
            eq[a * FOX_H + h, base + a] = 1.0
            eq[N_AUG * FOX_H + h, base + N_AUG + a] = 1.0
            ek[N_AUG * FOX_H + h, base + a] = 1.0
            ek[a * FOX_H + h, base + N_AUG + a] = -1.0
    return eq, ek


def _fox_proj(x, batch, gains, g, w_in, b_f):
    n, d = x.shape
    seq = n // batch
    ts = FOX_ROWS
    nt = seq // ts
    hd = FOX_H * HEAD_DIM
    wq = w_in[:, :hd]
    wk = w_in[:, hd:2 * hd].astype(BF16)
    wv = w_in[:, 2 * hd:3 * hd]
    reps = N_AUG
    wf = jnp.pad(jnp.tile(w_in[:, 3 * hd:], (1, reps)), ((0, 0), (0, LANES - reps * FOX_H))).astype(BF16)
    bf = jnp.pad(jnp.tile(b_f, reps), (0, LANES - reps * FOX_H)).reshape(1, LANES).astype(F32)
    tri = jnp.asarray(np.tril(np.ones((ts // 2, ts // 2), np.float32)), BF16)
    eq, ek = _fox_placement()
    eqt, ek = jnp.asarray(eq.T, BF16), jnp.asarray(ek, BF16)
    row = pl.BlockSpec((ts, d), lambda b, i: (b * nt + i, 0))
    wide = pl.BlockSpec((ts, FOX_H * LANES), lambda b, i: (b * nt + i, 0))
    tall = pl.BlockSpec((None, FOX_H * LANES, ts), lambda b, i: (b, 0, i))
    return pl.pallas_call(
        _fox_proj_body,
        grid=(batch, nt),
        in_specs=[row, _pick_spec(g, (1, d)), _const_spec((d, hd)), _const_spec((d, hd)),
                  _const_spec((d, hd)), _const_spec((d, LANES)), _const_spec((1, LANES)),
                  _const_spec(tri.shape), _const_spec(eqt.shape), _const_spec(ek.shape)],
        out_specs=[tall, wide, tall],
        out_shape=[jax.ShapeDtypeStruct((batch, FOX_H * LANES, seq), BF16),
                   jax.ShapeDtypeStruct((n, FOX_H * LANES), BF16),
                   jax.ShapeDtypeStruct((batch, FOX_H * LANES, seq), BF16)],
        scratch_shapes=[pltpu.VMEM((8, LANES), F32), pltpu.VMEM((hd, d), BF16), pltpu.VMEM((hd, d), BF16)],
        compiler_params=pltpu.CompilerParams(
            dimension_semantics=("arbitrary", "arbitrary"),
            vmem_limit_bytes=52 * MIB),
        name="fox_proj",
    )(x, gains, wq, wk, wv, wf, bf, tri, eqt, ek)


def _fox_attn_body(qt_ref, k_ref, vt_ref, o_ref):
    seq = k_ref.shape[0]
    tq = FOX_TQ
    tk = FOX_TK
    per_q = tq // tk
    row = lax.broadcasted_iota(jnp.int32, (LANES, tq), 0)
    chains = [(qi, hh) for qi in reversed(range(seq // tq)) for hh in range(2)]

    def queries_of(chain, kb):
        skip = tq - tk if kb == (chain[0] + 1) * per_q - 1 else 0
        return skip, tq - skip

    def scores(chain, kb):
        qi, hh = chain
        head = slice(hh * LANES, (hh + 1) * LANES)
        skip, nq = queries_of(chain, kb)
        q0 = qi * tq + skip
        st = _dot(k_ref[kb * tk:(kb + 1) * tk, head], qt_ref[head, q0:q0 + nq])
        if kb >= qi * per_q:
            key = lax.broadcasted_iota(jnp.int32, (tk, nq), 0)
            qry = lax.broadcasted_iota(jnp.int32, (tk, nq), 1)
            st = jnp.where(key + (kb * tk - q0) <= qry, st, NEG)
        return st

    def column_max(chain, sts):
        m = None
        for kb, st in enumerate(sts):
            skip, _ = queries_of(chain, kb)
            mb = jnp.max(st, axis=0, keepdims=True)
            if skip:
                mb = jnp.concatenate([m[:, :skip], jnp.maximum(m[:, skip:], mb)], axis=1)
                m = mb
            else:
                m = mb if m is None else jnp.maximum(m, mb)
        return m

    def probabilities(chain, kb, st, m):
        skip, _ = queries_of(chain, kb)
        return jnp.exp2(st - m[:, skip:]).astype(BF16)

    def weighted_values(chain, kb, pt, acc):
        _, hh = chain
        head = slice(hh * LANES, (hh + 1) * LANES)
        skip, _ = queries_of(chain, kb)
        part = _dot(vt_ref[head, kb * tk:(kb + 1) * tk], pt)
        if acc is None:
            return part
        if skip:
            return jnp.concatenate([acc[:, :skip], acc[:, skip:] + part], axis=1)
        return acc + part

    def blocks(stage):
        return (stage[0][0] + 1) * per_q if stage is not None else 0

    done = {}
    scored = None
    exped = None
    for cur in chains + [None, None]:
        n_cur = (cur[0] + 1) * per_q if cur is not None else 0
        sts, pts, acc = [], [], None
        for kb in range(max(n_cur, blocks(scored), blocks(exped))):
            if kb < n_cur:
                sts.append(scores(cur, kb))
            if kb < blocks(scored):
                pts.append(probabilities(scored[0], kb, scored[1][kb], scored[2]))
            if kb < blocks(exped):
                acc = weighted_values(exped[0], kb, exped[1][kb], acc)
        if exped is not None:
            qi, hh = exped[0]
            ones_row = HEAD_DIM if hh == 0 else 0
            done[hh] = acc / acc[ones_row:ones_row + 1, :]
            if hh == 1:
                out = jnp.where(row < HEAD_DIM, done[0], done[1])
                o_ref[qi * tq:(qi + 1) * tq, :] = out.T.astype(BF16)
        exped = (scored[0], pts) if scored is not None else None
        scored = (cur, sts, column_max(cur, sts)) if cur is not None else None


def _fox_attn(qt, k, vt, batch):
    n = k.shape[0]
    seq = n // batch
    pairs = FOX_H // 2
    k = k.reshape(batch, seq, FOX_H * LANES)
    pair_block = pl.BlockSpec((None, seq, 2 * LANES), lambda b, p: (b, 0, p))
    out = pl.pallas_call(
        _fox_attn_body,
        grid=(batch, pairs),
        in_specs=[pl.BlockSpec((None, 2 * LANES, seq), lambda b, p: (b, p, 0)), pair_block,
                  pl.BlockSpec((None, 2 * LANES, seq), lambda b, p: (b, p, 0))],
        out_specs=pl.BlockSpec((None, seq, LANES), lambda b, p: (b, 0, p)),
        out_shape=jax.ShapeDtypeStruct((batch, seq, FOX_H * HEAD_DIM), BF16),
        compiler_params=pltpu.CompilerParams(
            dimension_semantics=("parallel", "parallel"),
            vmem_limit_bytes=48 * MIB),
        name="fox_attn",
    )(qt, k, vt)
    return out.reshape(n, FOX_H * HEAD_DIM)


def kernel(x, norm_g, ffn_w_gate, ffn_w_up, ffn_w_down, ab_w_in, conv_w, conv_b, conv_ln_g,
           conv_ln_b, swa_sinks, ab_w_out, fox_w_in, fox_b_f, fox_w_out):
    batch, seq, d = x.shape
    depth = norm_g.shape[0]
    xf = x.reshape(batch * seq, d)
    stacks = (ffn_w_gate, ffn_w_up, ffn_w_down)
    w_next = tuple(w[0, 0].astype(BF16) for w in stacks)
    n_norms = norm_g.shape[1]
    gains = norm_g.reshape(depth * n_norms, 1, d)
    conv_b, conv_ln_g, conv_ln_b = (p.reshape(p.shape[0], 1, -1) for p in (conv_b, conv_ln_g, conv_ln_b))
    for l in range(depth):
        g = l * n_norms
        xf, w_next = _ffn(xf, gains, g, g + 1, w_next, cast=(stacks, (l, 1)))
        i = l // 2
        mix = None
        if l % 2 == 0:
            xf = _even_mixer(xf, batch, gains, g + 2, g + 3, i, ab_w_in, conv_w, conv_b,
                             conv_ln_g, conv_ln_b, swa_sinks[i], ab_w_out)
        else:
            qt, k, vt = _fox_proj(xf, batch, gains, g + 2, fox_w_in[i], fox_b_f[i])
            mix = (_fox_attn(qt, k, vt, batch), fox_w_out[i], g + 3)
        cast = (stacks, (l + 1, 0)) if l + 1 < depth else None
        xf, w_next = _ffn(xf, gains, g + 4, g + 5, w_next, mix, cast)
    return xf.reshape(batch, seq, d)
```

```python
import functools
import math

import jax
import jax.numpy as jnp
import numpy as np
from jax import lax
from jax.experimental import pallas as pl
from jax.experimental.pallas import tpu as pltpu

F32 = jnp.float32
BF16 = jnp.bfloat16

LANES = 128
V7X_VMEM_BYTES = 64 * 1024 * 1024
MIB = 1024 * 1024

D_MODEL = 1024
D_FF = 2816
EPS = 1e-6
HEAD_DIM = 64
CONV_CH = 512
CONV_K = 31
SWA_HQ = 8
SWA_HKV = 2
SWA_G = SWA_HQ // SWA_HKV
WINDOW = 128
BLK = 128
FOX_H = 16
NEG = -1e30

FFN_ROWS = 1024
FFN_SUB_ROWS = 512
FFN_CHUNK = 256
MIX_ROWS = 512
CONV_HALO = 32
CONV_PHASES = 4
FOX_ROWS = 512
FOX_TQ = 512
FOX_TK = 256
N_AUG = 3
LOG2E = 1.4426950408889634


def _rms(x, g):
    ms = jnp.mean(x * x, axis=-1, keepdims=True)
    return x * lax.rsqrt(ms + EPS) * g


def _dot(a, b):
    return jnp.dot(a, b, preferred_element_type=F32)


def _dot_nt(a, b):
    return lax.dot_general(a, b, (((1,), (1,)), ((), ())), preferred_element_type=F32)


def _split_bf16(x):
    p1 = x.astype(BF16)
    r1 = x - p1.astype(F32)
    p2 = r1.astype(BF16)
    r2 = r1 - p2.astype(F32)
    return p1, p2, r2.astype(BF16)


def _const_spec(shape):
    n = len(shape)
    return pl.BlockSpec(shape, lambda *_: (0,) * n, pipeline_mode=pl.Buffered(1))


def _pick_spec(index, shape):
    n = len(shape)
    return pl.BlockSpec((None,) + tuple(shape), lambda *_: (index,) + (0,) * n,
                        pipeline_mode=pl.Buffered(1))


def _ffn_body(*refs, mix_in, cast_next):
    if mix_in:
        attn_ref, wmix_ref, gmix_ref, *refs = refs
    if cast_next:
        (x_ref, gpre_ref, gpost_ref, wg_ref, wu_ref, wd_ref, ng32_ref, nu32_ref, nd32_ref,
         o_ref, ng_ref, nu_ref, nd_ref, a_ref) = refs
        casts = [(ng32_ref, ng_ref), (nu32_ref, nu_ref), (nd32_ref, nd_ref)]
    else:
        casts = []
        x_ref, gpre_ref, gpost_ref, wg_ref, wu_ref, wd_ref, o_ref, a_ref = refs
    tf = FFN_CHUNK
    for r0 in range(0, x_ref.shape[0], FFN_SUB_ROWS):
        rows = slice(r0, r0 + FFN_SUB_ROWS)
        x = x_ref[rows, :]
        if mix_in:
            x = x + _rms(_dot(attn_ref[rows, :], wmix_ref[...]), gmix_ref[...])
        h = _rms(x, gpre_ref[...]).astype(BF16)
        for c in range(wg_ref.shape[1] // tf):
            if casts and c % 2 == 1:
                src, dst = casts.pop()
                dst[...] = src[...].astype(BF16)
            cols = slice(c * tf, (c + 1) * tf)
            g = _dot(h, wg_ref[:, cols])
            u = _dot(h, wu_ref[:, cols])
            a_ref[rows, cols] = (g * jax.nn.sigmoid(g) * u).astype(BF16)
        y = _dot(a_ref[rows, :], wd_ref[...])
        o_ref[rows, :] = x + 0.5 * _rms(y, gpost_ref[...])


def _ffn(x, gains, pre, post, weights_bf16, mix=None, cast=None):
    n, d = x.shape
    tm = FFN_ROWS
    steps = n // tm
    row = pl.BlockSpec((tm, d), lambda i: (i, 0))
    args = [x, gains, gains, *weights_bf16]
    in_specs = [row, _pick_spec(pre, (1, d)), _pick_spec(post, (1, d)),
                _const_spec((d, D_FF)), _const_spec((d, D_FF)), _const_spec((D_FF, d))]
    out_specs = [row]
    out_shape = [jax.ShapeDtypeStruct((n, d), F32)]
    weights = 3 * d * D_FF * 2
    tiles = 4 * tm * d * 4 + tm * D_FF * 2 + 4 * tm * FFN_CHUNK * 4 + 2 * tm * d * 4
    if cast is not None:
        stacks, which = cast
        for w in stacks:
            rows, cols = w.shape[2] // steps, w.shape[3]
            args.append(w)
            in_specs.append(pl.BlockSpec((None, None, rows, cols), lambda i: which + (i, 0)))
            out_specs.append(pl.BlockSpec((rows, cols), lambda i: (i, 0)))
            out_shape.append(jax.ShapeDtypeStruct(w.shape[2:], BF16))
            tiles += rows * cols * (4 + 2)
    if mix is not None:
        attn, w_mix, g_mix = mix
        args = [attn, w_mix.astype(BF16), gains] + args
        in_specs = [row, _const_spec((d, d)), _pick_spec(g_mix, (1, d))] + in_specs
        weights += d * d * 2
        tiles += 2 * tm * d * 2 + tm * d * 4
    out = pl.pallas_call(
        functools.partial(_ffn_body, mix_in=mix is not None, cast_next=cast is not None),
        grid=(steps,),
        in_specs=in_specs,
        out_specs=out_specs,
        out_shape=out_shape,
        scratch_shapes=[pltpu.VMEM((tm, D_FF), BF16)],
        compiler_params=pltpu.CompilerParams(
            dimension_semantics=("parallel",),
            vmem_limit_bytes=min(V7X_VMEM_BYTES - 8 * MIB, weights + 2 * tiles)),
        name="ffn_mix" if mix is not None else "ffn",
    )(*args)
    return out[0], tuple(out[1:])


def _even_body(sinks_ref, x_ref, g2_ref, g3_ref, win32_ref, cw_ref, cb_ref, lng_ref, lnb_ref,
               bias_ref, wout32_ref, o_ref, a_buf, k_buf, vt_buf, y_buf, cat_buf, win_ref, wout_ref):
    i = pl.program_id(1)
    ts = x_ref.shape[0]
    n_slabs = CONV_CH // LANES
    chunk = 2 * LANES
    o_q = 2 * CONV_CH
    o_k = o_q + SWA_HQ * HEAD_DIM

    @pl.when(jnp.logical_and(pl.program_id(0) == 0, i == 0))
    def _():
        for cb in range(n_slabs):
            for part, half in enumerate((0, CONV_CH)):
                src = slice(half + cb * LANES, half + (cb + 1) * LANES)
                dst = slice(cb * chunk + part * LANES, cb * chunk + (part + 1) * LANES)
                win_ref[:, dst] = win32_ref[:, src].astype(BF16)
        scale = LOG2E / math.sqrt(HEAD_DIM)
        for j in range(SWA_G):
            heads = [win32_ref[:, o_q + g * HEAD_DIM:o_q + (g + 1) * HEAD_DIM] for g in (j, SWA_G + j)]
            win_ref[:, o_q + j * LANES:o_q + (j + 1) * LANES] = (
                jnp.concatenate(heads, axis=1) * scale).astype(BF16)
            for part, g in enumerate((j, SWA_G + j)):
                dst = CONV_CH + j * LANES + part * HEAD_DIM
                src = CONV_CH + g * HEAD_DIM
                wout_ref[dst:dst + HEAD_DIM, :] = wout32_ref[src:src + HEAD_DIM, :].astype(BF16)
        win_ref[:, o_k:] = win32_ref[:, o_k:].astype(BF16)
        wout_ref[0:CONV_CH, :] = wout32_ref[0:CONV_CH, :].astype(BF16)

    x = x_ref[...]
    h = _rms(x, g2_ref[...]).astype(BF16)

    @pl.when(i == 0)
    def _():
        a_buf[:, 0:CONV_HALO, :] = jnp.zeros((n_slabs, CONV_HALO, LANES), F32)
        k_buf[0:BLK, :] = jnp.zeros((BLK, LANES), BF16)
        vt_buf[:, 0:BLK] = jnp.zeros((LANES, BLK), BF16)

    @pl.when(i > 0)
    def _():
        a_buf[:, 0:CONV_HALO, :] = a_buf[:, ts:ts + CONV_HALO, :]
        k_buf[0:BLK, :] = k_buf[ts:ts + BLK, :]
        vt_buf[:, 0:BLK] = vt_buf[:, ts:ts + BLK]

    def project(c):
        return _dot(h, win_ref[:, c * chunk:(c + 1) * chunk])

    def gate_into_slab(cb, zc):
        a_buf[cb, CONV_HALO:CONV_HALO + ts, :] = zc[:, :LANES] * jax.nn.sigmoid(zc[:, LANES:])

    first_tap = CONV_HALO - (CONV_K - 1)
    n_rows = ts // CONV_PHASES

    def conv_phase(cb, r):
        cols = slice(cb * LANES, (cb + 1) * LANES)
        acc = jnp.broadcast_to(cb_ref[:, cols], (n_rows, LANES))
        for j in range(CONV_K):
            taps = a_buf[cb, pl.ds(first_tap + r + j, n_rows, stride=CONV_PHASES), :]
            acc = acc + cw_ref[j:j + 1, cols] * taps
        y_buf[cb, pl.ds(r, n_rows, stride=CONV_PHASES), :] = acc

    assert n_slabs == ts // BLK == 4
    gate_into_slab(0, project(0))
    zq = []
    for cb in range(n_slabs - 1):
        conv_phase(cb, 0)
        gate_into_slab(cb + 1, project(cb + 1))
        conv_phase(cb, 1)
        conv_phase(cb, 2)
        if cb < 2:
            zq.append(project(n_slabs + cb))
        else:
            zkv = project(n_slabs + 2)
            k_buf[BLK:BLK + ts, :] = zkv[:, :LANES].astype(BF16)
            vt_buf[:, BLK:BLK + ts] = zkv[:, LANES:].T.astype(BF16)
        conv_phase(cb, 3)

    lane = lax.broadcasted_iota(jnp.int32, (BLK, LANES), 1)
    key_is_prev = lax.broadcasted_iota(jnp.int32, (2 * BLK, SWA_G * BLK), 0) < BLK
    units = [(qb, hk) for qb in range(ts // BLK) for hk in range(SWA_HKV)]
    last = n_slabs - 1

    def unit_scores(qb, hk):
        rows = slice(qb * BLK, (qb + 1) * BLK)
        mine = (lane < HEAD_DIM) if hk == 0 else (lane >= HEAD_DIM)
        qs = jnp.concatenate(
            [jnp.where(mine, zq[j // 2][rows, (j % 2) * LANES:(j % 2 + 1) * LANES], 0.0)
             for j in range(SWA_G)], axis=0).astype(BF16)
        st = _dot_nt(k_buf[qb * BLK:qb * BLK + 2 * BLK, :], qs)
        bias = bias_ref[hk]
        if qb == 0:
            bias = jnp.where(jnp.logical_and(i == 0, key_is_prev), NEG, bias)
        return st + bias

    def unit_softmax(hk, st):
        sink = jnp.concatenate(
            [jnp.full((1, BLK), sinks_ref[SWA_G * hk + j], F32) for j in range(SWA_G)], axis=1)
        m = jnp.maximum(jnp.max(st, axis=0, keepdims=True), sink)
        p = jnp.exp2(st - m)
        den = jnp.sum(p, axis=0, keepdims=True) + jnp.exp2(sink - m)
        return p.astype(BF16), den

    def unit_values(qb, p, den):
        return _dot(vt_buf[:, qb * BLK:qb * BLK + 2 * BLK], p) / den

    conv_phase(last, 0)
    sts = [unit_scores(qb, hk) for qb, hk in units]
    conv_phase(last, 1)
    pds = [unit_softmax(hk, st) for (qb, hk), st in zip(units, sts)]
    conv_phase(last, 2)
    ots = [unit_values(qb, p, den) for (qb, hk), (p, den) in zip(units, pds)]
    conv_phase(last, 3)
    row = lax.broadcasted_iota(jnp.int32, (LANES, SWA_G * BLK), 0)
    for qb in range(ts // BLK):
        out = jnp.where(row < HEAD_DIM, ots[SWA_HKV * qb], ots[SWA_HKV * qb + 1]).T
        for j in range(SWA_G):
            cat_buf[qb * BLK:(qb + 1) * BLK, CONV_CH + j * LANES:CONV_CH + (j + 1) * LANES] = (
                out[j * BLK:(j + 1) * BLK].astype(BF16))

    ys = [y_buf[cb] for cb in range(n_slabs)]
    mu = jnp.sum(sum(ys), axis=-1, keepdims=True) * (1.0 / CONV_CH)
    ycs = [y - mu for y in ys]
    var = jnp.sum(sum(yc * yc for yc in ycs), axis=-1, keepdims=True) * (1.0 / CONV_CH)
    inv = lax.rsqrt(var + EPS)
    for cb in range(n_slabs):
        cols = slice(cb * LANES, (cb + 1) * LANES)
        yn = ycs[cb] * inv * lng_ref[:, cols] + lnb_ref[:, cols]
        cat_buf[:, cols] = (yn * jax.nn.sigmoid(yn)).astype(BF16)

    y2 = _dot(cat_buf[...], wout_ref[...])
    o_ref[...] = x + _rms(y2, g3_ref[...])


def _swa_bias_table():
    t_loc = np.arange(BLK)[None, :]
    s_loc = np.arange(2 * BLK)[:, None]
    dist = t_loc + BLK - s_loc
    valid = (dist >= 0) & (dist < WINDOW)
    slopes = [2.0 ** (-8.0 * (g + 1) / SWA_HQ) for g in range(SWA_HQ)]
    per_head = [np.where(valid, -np.float32(sl * LOG2E) * dist.astype(np.float32), np.float32(NEG))
                for sl in slopes]
    return np.stack([np.concatenate(per_head[SWA_G * hk:SWA_G * (hk + 1)], axis=1)
                     for hk in range(SWA_HKV)]).astype(np.float32)


def _even_mixer(x, batch, gains, g2, g3, layer, w_in, conv_w, conv_b, ln_g, ln_b, sinks, w_out):
    n, d = x.shape
    seq = n // batch
    ts = MIX_ROWS
    nt = seq // ts
    ab_in = w_in.shape[2]
    bias = _swa_bias_table()
    row = pl.BlockSpec((ts, d), lambda b, i, *_: (b * nt + i, 0))

    def const(shape):
        k = len(shape)
        return pl.BlockSpec(shape, lambda b, i, *_: (0,) * k, pipeline_mode=pl.Buffered(1))

    grid_spec = pltpu.PrefetchScalarGridSpec(
        num_scalar_prefetch=1,
        grid=(batch, nt),
        in_specs=[row, _pick_spec(g2, (1, d)), _pick_spec(g3, (1, d)), _pick_spec(layer, (d, ab_in)),
                  _pick_spec(layer, (CONV_K, CONV_CH)), _pick_spec(layer, (1, CONV_CH)),
                  _pick_spec(layer, (1, CONV_CH)), _pick_spec(layer, (1, CONV_CH)),
                  const(bias.shape), _pick_spec(layer, (d, d))],
        out_specs=row,
        scratch_shapes=[pltpu.VMEM((CONV_CH // LANES, CONV_HALO + ts, LANES), F32),
                        pltpu.VMEM((BLK + ts, LANES), BF16),
                        pltpu.VMEM((LANES, BLK + ts), BF16),
                        pltpu.VMEM((CONV_CH // LANES, ts, LANES), F32),
                        pltpu.VMEM((ts, d), BF16),
                        pltpu.VMEM((d, ab_in), BF16),
                        pltpu.VMEM((d, d), BF16)])
    return pl.pallas_call(
        _even_body,
        grid_spec=grid_spec,
        out_shape=jax.ShapeDtypeStruct((n, d), F32),
        compiler_params=pltpu.CompilerParams(
            dimension_semantics=("arbitrary", "arbitrary"),
            vmem_limit_bytes=52 * MIB),
        name="even_mixer",
    )(sinks * LOG2E, x, gains, gains, w_in, conv_w, conv_b, ln_g, ln_b, bias, w_out)


def _fox_proj_body(x_ref, g_ref, win_ref, wf_ref, bf_ref, tri_ref, eqt_ref, ek_ref,
                   qt_ref, k_ref, vt_ref, carry_ref, wqt_ref, wk_ref, wvt_ref):
    i = pl.program_id(1)
    ts = x_ref.shape[0]

    @pl.when(i == 0)
    def _():
        carry_ref[...] = jnp.zeros(carry_ref.shape, F32)

    @pl.when(jnp.logical_and(pl.program_id(0) == 0, i == 0))
    def _():
        hd = wk_ref.shape[1]
        for c0 in range(0, hd, 2 * LANES):
            cols = slice(c0, c0 + 2 * LANES)
            wq = win_ref[:, c0:c0 + 2 * LANES] * (LOG2E / math.sqrt(HEAD_DIM))
            wqt_ref[cols, :] = wq.T.astype(BF16)
            wk_ref[:, cols] = win_ref[:, hd + c0:hd + c0 + 2 * LANES].astype(BF16)
            wvt_ref[cols, :] = win_ref[:, 2 * hd + c0:2 * hd + c0 + 2 * LANES].T.astype(BF16)

    h = _rms(x_ref[...], g_ref[...]).astype(BF16)

    t = _dot(h, wf_ref[...]) + bf_ref[...]
    zqt = _dot_nt(wqt_ref[...], h)
    logf = jnp.minimum(t, 0.0) - jnp.log1p(jnp.exp(-jnp.abs(t)))
    tri = tri_ref[...]
    half = ts // 2
    total = carry_ref[0:1, :]
    halves = []
    for r0 in (0, half):
        ch = total
        for piece in _split_bf16(logf[r0:r0 + half]):
            ch = ch + _dot(tri, piece)
        total = ch[half - 1:half, :]
        halves.append(ch)
    c = jnp.concatenate(halves, axis=0)
    carry_ref[0:1, :] = total
    zk = _dot(h, wk_ref[...])

    c1, c2, c3 = _split_bf16(c * LOG2E)
    lane = lax.broadcasted_iota(jnp.int32, (ts, LANES), 1)
    cp = jnp.where(lane < FOX_H, c1,
                   jnp.where(lane < 2 * FOX_H, c2,
                             jnp.where(lane < 3 * FOX_H, c3,
                                       jnp.where(lane < 4 * FOX_H, 1.0, 0.0).astype(BF16))))
    aqt = _dot_nt(eqt_ref[...], cp)
    ak = _dot(cp, ek_ref[...])
    zvt = _dot_nt(wvt_ref[...], h)
    low = lane < HEAD_DIM
    for p in range(FOX_H // 2):
        pair = slice(p * LANES, (p + 1) * LANES)
        even = slice(2 * p * LANES, (2 * p + 1) * LANES)
        odd = slice((2 * p + 1) * LANES, (2 * p + 2) * LANES)
        k_ref[:, even] = jnp.where(low, zk[:, pair], ak[:, pair]).astype(BF16)
        k_ref[:, odd] = jnp.where(low, ak[:, pair], zk[:, pair]).astype(BF16)
    row = lax.broadcasted_iota(jnp.int32, (LANES, ts), 0)
    top = row < HEAD_DIM
    one_mid = (row == HEAD_DIM).astype(F32)
    one_top = (row == 0).astype(F32)
    for p in range(FOX_H // 2):
        pair = slice(p * LANES, (p + 1) * LANES)
        even = slice(2 * p * LANES, (2 * p + 1) * LANES)
        odd = slice((2 * p + 1) * LANES, (2 * p + 2) * LANES)
        vt_ref[even, :] = jnp.where(top, zvt[pair, :], one_mid).astype(BF16)
        vt_ref[odd, :] = jnp.where(top, one_top, zvt[pair, :]).astype(BF16)
        qt_ref[even, :] = jnp.where(top, zqt[pair, :], aqt[pair, :]).astype(BF16)
        qt_ref[odd, :] = jnp.where(top, aqt[pair, :], zqt[pair, :]).astype(BF16)


def _fox_placement():
    eq = np.zeros((LANES, FOX_H // 2 * LANES), np.float32)
    ek = np.zeros((LANES, FOX_H // 2 * LANES), np.float32)
    for h in range(FOX_H):
        base = (h // 2) * LANES + (HEAD_DIM if h % 2 == 0 else 0)
        for a in range(N_AUG):
            eq[a * FOX_H + h, base + a] = 1.0
            eq[N_AUG * FOX_H + h, base + N_AUG + a] = 1.0
            ek[N_AUG * FOX_H + h, base + a] = 1.0
            ek[a * FOX_H + h, base + N_AUG + a] = -1.0
    return eq, ek


def _fox_proj(x, batch, gains, g, layer, w_in_stack, b_f):
    n, d = x.shape
    w_in = w_in_stack[layer]
    seq = n // batch
    ts = FOX_ROWS
    nt = seq // ts
    hd = FOX_H * HEAD_DIM
    reps = N_AUG
    wf = jnp.pad(jnp.tile(w_in[:, 3 * hd:], (1, reps)), ((0, 0), (0, LANES - reps * FOX_H))).astype(BF16)
    bf = jnp.pad(jnp.tile(b_f, reps), (0, LANES - reps * FOX_H)).reshape(1, LANES).astype(F32)
    tri = jnp.asarray(np.tril(np.ones((ts // 2, ts // 2), np.float32)), BF16)
    eq, ek = _fox_placement()
    eqt, ek = jnp.asarray(eq.T, BF16), jnp.asarray(ek, BF16)
    row = pl.BlockSpec((ts, d), lambda b, i: (b * nt + i, 0))
    wide = pl.BlockSpec((ts, FOX_H * LANES), lambda b, i: (b * nt + i, 0))
    tall = pl.BlockSpec((None, FOX_H * LANES, ts), lambda b, i: (b, 0, i))
    return pl.pallas_call(
        _fox_proj_body,
        grid=(batch, nt),
        in_specs=[row, _pick_spec(g, (1, d)), _pick_spec(layer, w_in.shape),
                  _const_spec((d, LANES)), _const_spec((1, LANES)),
                  _const_spec(tri.shape), _const_spec(eqt.shape), _const_spec(ek.shape)],
        out_specs=[tall, wide, tall],
        out_shape=[jax.ShapeDtypeStruct((batch, FOX_H * LANES, seq), BF16),
                   jax.ShapeDtypeStruct((n, FOX_H * LANES), BF16),
                   jax.ShapeDtypeStruct((batch, FOX_H * LANES, seq), BF16)],
        scratch_shapes=[pltpu.VMEM((8, LANES), F32), pltpu.VMEM((hd, d), BF16),
                        pltpu.VMEM((d, hd), BF16), pltpu.VMEM((hd, d), BF16)],
        compiler_params=pltpu.CompilerParams(
            dimension_semantics=("arbitrary", "arbitrary"),
            vmem_limit_bytes=56 * MIB),
        name="fox_proj",
    )(x, gains, w_in_stack, wf, bf, tri, eqt, ek)


def _fox_attn_body(qt_ref, k_ref, vt_ref, o_ref):
    seq = k_ref.shape[0]
    tq = FOX_TQ
    tk = FOX_TK
    per_q = tq // tk
    row = lax.broadcasted_iota(jnp.int32, (LANES, tq), 0)
    chains = [(qi, hh) for qi in reversed(range(seq // tq)) for hh in range(2)]

    def queries_of(chain, kb):
        skip = tq - tk if kb == (chain[0] + 1) * per_q - 1 else 0
        return skip, tq - skip

    def scores(chain, kb):
        qi, hh = chain
        head = slice(hh * LANES, (hh + 1) * LANES)
        skip, nq = queries_of(chain, kb)
        q0 = qi * tq + skip
        st = _dot(k_ref[kb * tk:(kb + 1) * tk, head], qt_ref[head, q0:q0 + nq])
        if kb >= qi * per_q:
            key = lax.broadcasted_iota(jnp.int32, (tk, nq), 0)
            qry = lax.broadcasted_iota(jnp.int32, (tk, nq), 1)
            st = jnp.where(key + (kb * tk - q0) <= qry, st, NEG)
        return st

    def column_max(chain, sts):
        m = None
        for kb, st in enumerate(sts):
            skip, _ = queries_of(chain, kb)
            mb = jnp.max(st, axis=0, keepdims=True)
            if skip:
                mb = jnp.concatenate([m[:, :skip], jnp.maximum(m[:, skip:], mb)], axis=1)
                m = mb
            else:
                m = mb if m is None else jnp.maximum(m, mb)
        return m

    def probabilities(chain, kb, st, m):
        skip, _ = queries_of(chain, kb)
        return jnp.exp2(st - m[:, skip:]).astype(BF16)

    def weighted_values(chain, kb, pt, acc):
        _, hh = chain
        head = slice(hh * LANES, (hh + 1) * LANES)
        skip, _ = queries_of(chain, kb)
        part = _dot(vt_ref[head, kb * tk:(kb + 1) * tk], pt)
        if acc is None:
            return part
        if skip:
            return jnp.concatenate([acc[:, :skip], acc[:, skip:] + part], axis=1)
        return acc + part

    def blocks(stage):
        return (stage[0][0] + 1) * per_q if stage is not None else 0

    done = {}
    scored = None
    exped = None
    for cur in chains + [None, None]:
        n_cur = (cur[0] + 1) * per_q if cur is not None else 0
        sts, pts, acc = [], [], None
        for kb in range(max(n_cur, blocks(scored), blocks(exped))):
            if kb < n_cur:
                sts.append(scores(cur, kb))
            if kb < blocks(scored):
                pts.append(probabilities(scored[0], kb, scored[1][kb], scored[2]))
            if kb < blocks(exped):
                acc = weighted_values(exped[0], kb, exped[1][kb], acc)
        if exped is not None:
            qi, hh = exped[0]
            ones_row = HEAD_DIM if hh == 0 else 0
            done[hh] = acc / acc[ones_row:ones_row + 1, :]
            if hh == 1:
                out = jnp.where(row < HEAD_DIM, done[0], done[1])
                o_ref[qi * tq:(qi + 1) * tq, :] = out.T.astype(BF16)
        exped = (scored[0], pts) if scored is not None else None
        scored = (cur, sts, column_max(cur, sts)) if cur is not None else None


def _fox_attn(qt, k, vt, batch):
    n = k.shape[0]
    seq = n // batch
    pairs = FOX_H // 2
    k = k.reshape(batch, seq, FOX_H * LANES)
    pair_block = pl.BlockSpec((None, seq, 2 * LANES), lambda b, p: (b, 0, p))
    out = pl.pallas_call(
        _fox_attn_body,
        grid=(batch, pairs),
        in_specs=[pl.BlockSpec((None, 2 * LANES, seq), lambda b, p: (b, p, 0)), pair_block,
                  pl.BlockSpec((None, 2 * LANES, seq), lambda b, p: (b, p, 0))],
        out_specs=pl.BlockSpec((None, seq, LANES), lambda b, p: (b, 0, p)),
        out_shape=jax.ShapeDtypeStruct((batch, seq, FOX_H * HEAD_DIM), BF16),
        compiler_params=pltpu.CompilerParams(
            dimension_semantics=("parallel", "parallel"),
            vmem_limit_bytes=48 * MIB),
        name="fox_attn",
    )(qt, k, vt)
    return out.reshape(n, FOX_H * HEAD_DIM)


def kernel(x, norm_g, ffn_w_gate, ffn_w_up, ffn_w_down, ab_w_in, conv_w, conv_b, conv_ln_g,
           conv_ln_b, swa_sinks, ab_w_out, fox_w_in, fox_b_f, fox_w_out):
    batch, seq, d = x.shape
    depth = norm_g.shape[0]
    xf = x.reshape(batch * seq, d)
    stacks = (ffn_w_gate, ffn_w_up, ffn_w_down)
    w_next = tuple(w[0, 0].astype(BF16) for w in stacks)
    n_norms = norm_g.shape[1]
    gains = norm_g.reshape(depth * n_norms, 1, d)
    conv_b, conv_ln_g, conv_ln_b = (p.reshape(p.shape[0], 1, -1) for p in (conv_b, conv_ln_g, conv_ln_b))
    for l in range(depth):
        g = l * n_norms
        xf, w_next = _ffn(xf, gains, g, g + 1, w_next, cast=(stacks, (l, 1)))
        i = l // 2
        mix = None
        if l % 2 == 0:
            xf = _even_mixer(xf, batch, gains, g + 2, g + 3, i, ab_w_in, conv_w, conv_b,
                             conv_ln_g, conv_ln_b, swa_sinks[i], ab_w_out)
        else:
            qt, k, vt = _fox_proj(xf, batch, gains, g + 2, i, fox_w_in, fox_b_f[i])
            mix = (_fox_attn(qt, k, vt, batch), fox_w_out[i], g + 3)
        cast = (stacks, (l + 1, 0)) if l + 1 < depth else None
        xf, w_next = _ffn(xf, gains, g + 4, g + 5, w_next, mix, cast)
    return xf.reshape(batch, seq, d)
```

```python
import functools
import math

import jax
import jax.numpy as jnp
import numpy as np
from jax import lax
from jax.experimental import pallas as pl
from jax.experimental.pallas import tpu as pltpu

F32 = jnp.float32
BF16 = jnp.bfloat16

LANES = 128
V7X_VMEM_BYTES = 64 * 1024 * 1024
MIB = 1024 * 1024

D_MODEL = 1024
D_FF = 2816
EPS = 1e-6
HEAD_DIM = 64
CONV_CH = 512
CONV_K = 31
SWA_HQ = 8
SWA_HKV = 2
SWA_G = SWA_HQ // SWA_HKV
WINDOW = 128
BLK = 128
FOX_H = 16
NEG = -1e30

FFN_ROWS = 1024
FFN_SUB_ROWS = 512
FFN_CHUNK = 256
MIX_ROWS = 512
CONV_HALO = 32
CONV_PHASES = 4
FOX_ROWS = 512
FOX_TQ = 512
FOX_TK = 256
N_AUG = 3
LOG2E = 1.4426950408889634


def _rms(x, g):
    ms = jnp.mean(x * x, axis=-1, keepdims=True)
    return x * lax.rsqrt(ms + EPS) * g


def _dot(a, b):
    return jnp.dot(a, b, preferred_element_type=F32)


def _dot_nt(a, b):
    return lax.dot_general(a, b, (((1,), (1,)), ((), ())), preferred_element_type=F32)


def _split_bf16(x):
    p1 = x.astype(BF16)
    r1 = x - p1.astype(F32)
    p2 = r1.astype(BF16)
    r2 = r1 - p2.astype(F32)
    return p1, p2, r2.astype(BF16)


def _const_spec(shape):
    n = len(shape)
    return pl.BlockSpec(shape, lambda *_: (0,) * n, pipeline_mode=pl.Buffered(1))


def _pick_spec(index, shape):
    n = len(shape)
    return pl.BlockSpec((None,) + tuple(shape), lambda *_: (index,) + (0,) * n,
                        pipeline_mode=pl.Buffered(1))


def _ffn_body(*refs, mix_in, cast_next):
    if mix_in:
        attn_ref, wmix_ref, gmix_ref, *refs = refs
    if cast_next:
        (x_ref, gpre_ref, gpost_ref, wg_ref, wu_ref, wd_ref, ng32_ref, nu32_ref, nd32_ref,
         o_ref, ng_ref, nu_ref, nd_ref, a_ref) = refs
        casts = [(ng32_ref, ng_ref), (nu32_ref, nu_ref), (nd32_ref, nd_ref)]
    else:
        casts = []
        x_ref, gpre_ref, gpost_ref, wg_ref, wu_ref, wd_ref, o_ref, a_ref = refs
    tf = FFN_CHUNK
    for r0 in range(0, x_ref.shape[0], FFN_SUB_ROWS):
        rows = slice(r0, r0 + FFN_SUB_ROWS)
        x = x_ref[rows, :]
        if mix_in:
            x = x + _rms(_dot(attn_ref[rows, :], wmix_ref[...]), gmix_ref[...])
        h = _rms(x, gpre_ref[...]).astype(BF16)
        for c in range(wg_ref.shape[1] // tf):
            if casts and c % 2 == 1:
                src, dst = casts.pop()
                dst[...] = src[...].astype(BF16)
            cols = slice(c * tf, (c + 1) * tf)
            gu = _dot(h, jnp.concatenate([wg_ref[:, cols], wu_ref[:, cols]], axis=1))
            g, u = gu[:, :tf], gu[:, tf:]
            a_ref[rows, cols] = (g * jax.nn.sigmoid(g) * u).astype(BF16)
        y = _dot(a_ref[rows, :], wd_ref[...])
        o_ref[rows, :] = x + 0.5 * _rms(y, gpost_ref[...])


def _ffn(x, gains, pre, post, weights_bf16, mix=None, cast=None):
    n, d = x.shape
    tm = FFN_ROWS
    steps = n // tm
    row = pl.BlockSpec((tm, d), lambda i: (i, 0))
    args = [x, gains, gains, *weights_bf16]
    in_specs = [row, _pick_spec(pre, (1, d)), _pick_spec(post, (1, d)),
                _const_spec((d, D_FF)), _const_spec((d, D_FF)), _const_spec((D_FF, d))]
    out_specs = [row]
    out_shape = [jax.ShapeDtypeStruct((n, d), F32)]
    weights = 3 * d * D_FF * 2
    tiles = 4 * tm * d * 4 + tm * D_FF * 2 + 4 * tm * FFN_CHUNK * 4 + 2 * tm * d * 4
    if cast is not None:
        stacks, which = cast
        for w in stacks:
            rows, cols = w.shape[2] // steps, w.shape[3]
            args.append(w)
            in_specs.append(pl.BlockSpec((None, None, rows, cols), lambda i: which + (i, 0)))
            out_specs.append(pl.BlockSpec((rows, cols), lambda i: (i, 0)))
            out_shape.append(jax.ShapeDtypeStruct(w.shape[2:], BF16))
            tiles += rows * cols * (4 + 2)
    if mix is not None:
        attn, w_mix, g_mix = mix
        args = [attn, w_mix.astype(BF16), gains] + args
        in_specs = [row, _const_spec((d, d)), _pick_spec(g_mix, (1, d))] + in_specs
        weights += d * d * 2
        tiles += 2 * tm * d * 2 + tm * d * 4
    out = pl.pallas_call(
        functools.partial(_ffn_body, mix_in=mix is not None, cast_next=cast is not None),
        grid=(steps,),
        in_specs=in_specs,
        out_specs=out_specs,
        out_shape=out_shape,
        scratch_shapes=[pltpu.VMEM((tm, D_FF), BF16)],
        compiler_params=pltpu.CompilerParams(
            dimension_semantics=("parallel",),
            vmem_limit_bytes=min(V7X_VMEM_BYTES - 8 * MIB, weights + 2 * tiles)),
        name="ffn_mix" if mix is not None else "ffn",
    )(*args)
    return out[0], tuple(out[1:])


def _even_body(sinks_ref, x_ref, g2_ref, g3_ref, win32_ref, cw_ref, cb_ref, lng_ref, lnb_ref,
               bias_ref, wout32_ref, o_ref, a_buf, k_buf, vt_buf, y_buf, cat_buf, win_ref, wout_ref):
    i = pl.program_id(1)
    ts = x_ref.shape[0]
    n_slabs = CONV_CH // LANES
    chunk = 2 * LANES
    o_q = 2 * CONV_CH
    o_k = o_q + SWA_HQ * HEAD_DIM

    @pl.when(jnp.logical_and(pl.program_id(0) == 0, i == 0))
    def _():
        for cb in range(n_slabs):
            for part, half in enumerate((0, CONV_CH)):
                src = slice(half + cb * LANES, half + (cb + 1) * LANES)
                dst = slice(cb * chunk + part * LANES, cb * chunk + (part + 1) * LANES)
                win_ref[:, dst] = win32_ref[:, src].astype(BF16)
        scale = LOG2E / math.sqrt(HEAD_DIM)
        for j in range(SWA_G):
            heads = [win32_ref[:, o_q + g * HEAD_DIM:o_q + (g + 1) * HEAD_DIM] for g in (j, SWA_G + j)]
            win_ref[:, o_q + j * LANES:o_q + (j + 1) * LANES] = (
                jnp.concatenate(heads, axis=1) * scale).astype(BF16)
            for part, g in enumerate((j, SWA_G + j)):
                dst = CONV_CH + j * LANES + part * HEAD_DIM
                src = CONV_CH + g * HEAD_DIM
                wout_ref[dst:dst + HEAD_DIM, :] = wout32_ref[src:src + HEAD_DIM, :].astype(BF16)
        win_ref[:, o_k:] = win32_ref[:, o_k:].astype(BF16)
        wout_ref[0:CONV_CH, :] = wout32_ref[0:CONV_CH, :].astype(BF16)

    x = x_ref[...]
    h = _rms(x, g2_ref[...]).astype(BF16)

    @pl.when(i == 0)
    def _():
        a_buf[:, 0:CONV_HALO, :] = jnp.zeros((n_slabs, CONV_HALO, LANES), F32)
        k_buf[0:BLK, :] = jnp.zeros((BLK, LANES), BF16)
        vt_buf[:, 0:BLK] = jnp.zeros((LANES, BLK), BF16)

    @pl.when(i > 0)
    def _():
        a_buf[:, 0:CONV_HALO, :] = a_buf[:, ts:ts + CONV_HALO, :]
        k_buf[0:BLK, :] = k_buf[ts:ts + BLK, :]
        vt_buf[:, 0:BLK] = vt_buf[:, ts:ts + BLK]

    def project(c):
        return _dot(h, win_ref[:, c * chunk:(c + 1) * chunk])

    def gate_into_slab(cb, zc):
        a_buf[cb, CONV_HALO:CONV_HALO + ts, :] = zc[:, :LANES] * jax.nn.sigmoid(zc[:, LANES:])

    first_tap = CONV_HALO - (CONV_K - 1)
    n_rows = ts // CONV_PHASES

    def conv_phase(cb, r):
        cols = slice(cb * LANES, (cb + 1) * LANES)
        acc = jnp.broadcast_to(cb_ref[:, cols], (n_rows, LANES))
        for j in range(CONV_K):
            taps = a_buf[cb, pl.ds(first_tap + r + j, n_rows, stride=CONV_PHASES), :]
            acc = acc + cw_ref[j:j + 1, cols] * taps
        y_buf[cb, pl.ds(r, n_rows, stride=CONV_PHASES), :] = acc

    assert n_slabs == ts // BLK == 4
    gate_into_slab(0, project(0))
    zq = []
    for cb in range(n_slabs - 1):
        conv_phase(cb, 0)
        gate_into_slab(cb + 1, project(cb + 1))
        conv_phase(cb, 1)
        conv_phase(cb, 2)
        if cb < 2:
            zq.append(project(n_slabs + cb))
        else:
            zkv = project(n_slabs + 2)
            k_buf[BLK:BLK + ts, :] = zkv[:, :LANES].astype(BF16)
            vt_buf[:, BLK:BLK + ts] = zkv[:, LANES:].T.astype(BF16)
        conv_phase(cb, 3)

    lane = lax.broadcasted_iota(jnp.int32, (BLK, LANES), 1)
    key_is_prev = lax.broadcasted_iota(jnp.int32, (2 * BLK, SWA_G * BLK), 0) < BLK
    units = [(qb, hk) for qb in range(ts // BLK) for hk in range(SWA_HKV)]
    last = n_slabs - 1

    def unit_scores(qb, hk):
        rows = slice(qb * BLK, (qb + 1) * BLK)
        mine = (lane < HEAD_DIM) if hk == 0 else (lane >= HEAD_DIM)
        qs = jnp.concatenate(
            [jnp.where(mine, zq[j // 2][rows, (j % 2) * LANES:(j % 2 + 1) * LANES], 0.0)
             for j in range(SWA_G)], axis=0).astype(BF16)
        st = _dot_nt(k_buf[qb * BLK:qb * BLK + 2 * BLK, :], qs)
        bias = bias_ref[hk]
        if qb == 0:
            bias = jnp.where(jnp.logical_and(i == 0, key_is_prev), NEG, bias)
        return st + bias

    def unit_softmax(hk, st):
        sink = jnp.concatenate(
            [jnp.full((1, BLK), sinks_ref[SWA_G * hk + j], F32) for j in range(SWA_G)], axis=1)
        m = jnp.maximum(jnp.max(st, axis=0, keepdims=True), sink)
        p = jnp.exp2(st - m)
        den = jnp.sum(p, axis=0, keepdims=True) + jnp.exp2(sink - m)
        return p.astype(BF16), den

    def unit_values(qb, p, den):
        return _dot(vt_buf[:, qb * BLK:qb * BLK + 2 * BLK], p) / den

    conv_phase(last, 0)
    sts = [unit_scores(qb, hk) for qb, hk in units]
    conv_phase(last, 1)
    pds = [unit_softmax(hk, st) for (qb, hk), st in zip(units, sts)]
    conv_phase(last, 2)
    ots = [unit_values(qb, p, den) for (qb, hk), (p, den) in zip(units, pds)]
    conv_phase(last, 3)
    row = lax.broadcasted_iota(jnp.int32, (LANES, SWA_G * BLK), 0)
    for qb in range(ts // BLK):
        out = jnp.where(row < HEAD_DIM, ots[SWA_HKV * qb], ots[SWA_HKV * qb + 1]).T
        for j in range(SWA_G):
            cat_buf[qb * BLK:(qb + 1) * BLK, CONV_CH + j * LANES:CONV_CH + (j + 1) * LANES] = (
                out[j * BLK:(j + 1) * BLK].astype(BF16))

    ys = [y_buf[cb] for cb in range(n_slabs)]
    mu = jnp.sum(sum(ys), axis=-1, keepdims=True) * (1.0 / CONV_CH)
    ycs = [y - mu for y in ys]
    var = jnp.sum(sum(yc * yc for yc in ycs), axis=-1, keepdims=True) * (1.0 / CONV_CH)
    inv = lax.rsqrt(var + EPS)
    for cb in range(n_slabs):
        cols = slice(cb * LANES, (cb + 1) * LANES)
        yn = ycs[cb] * inv * lng_ref[:, cols] + lnb_ref[:, cols]
        cat_buf[:, cols] = (yn * jax.nn.sigmoid(yn)).astype(BF16)

    y2 = _dot(cat_buf[...], wout_ref[...])
    o_ref[...] = x + _rms(y2, g3_ref[...])


def _swa_bias_table():
    t_loc = np.arange(BLK)[None, :]
    s_loc = np.arange(2 * BLK)[:, None]
    dist = t_loc + BLK - s_loc
    valid = (dist >= 0) & (dist < WINDOW)
    slopes = [2.0 ** (-8.0 * (g + 1) / SWA_HQ) for g in range(SWA_HQ)]
    per_head = [np.where(valid, -np.float32(sl * LOG2E) * dist.astype(np.float32), np.float32(NEG))
                for sl in slopes]
    return np.stack([np.concatenate(per_head[SWA_G * hk:SWA_G * (hk + 1)], axis=1)
                     for hk in range(SWA_HKV)]).astype(np.float32)


def _even_mixer(x, batch, gains, g2, g3, layer, w_in, conv_w, conv_b, ln_g, ln_b, sinks, w_out):
    n, d = x.shape
    seq = n // batch
    ts = MIX_ROWS
    nt = seq // ts
    ab_in = w_in.shape[2]
    bias = _swa_bias_table()
    row = pl.BlockSpec((ts, d), lambda b, i, *_: (b * nt + i, 0))

    def const(shape):
        k = len(shape)
        return pl.BlockSpec(shape, lambda b, i, *_: (0,) * k, pipeline_mode=pl.Buffered(1))

    grid_spec = pltpu.PrefetchScalarGridSpec(
        num_scalar_prefetch=1,
        grid=(batch, nt),
        in_specs=[row, _pick_spec(g2, (1, d)), _pick_spec(g3, (1, d)), _pick_spec(layer, (d, ab_in)),
                  _pick_spec(layer, (CONV_K, CONV_CH)), _pick_spec(layer, (1, CONV_CH)),
                  _pick_spec(layer, (1, CONV_CH)), _pick_spec(layer, (1, CONV_CH)),
                  const(bias.shape), _pick_spec(layer, (d, d))],
        out_specs=row,
        scratch_shapes=[pltpu.VMEM((CONV_CH // LANES, CONV_HALO + ts, LANES), F32),
                        pltpu.VMEM((BLK + ts, LANES), BF16),
                        pltpu.VMEM((LANES, BLK + ts), BF16),
                        pltpu.VMEM((CONV_CH // LANES, ts, LANES), F32),
                        pltpu.VMEM((ts, d), BF16),
                        pltpu.VMEM((d, ab_in), BF16),
                        pltpu.VMEM((d, d), BF16)])
    return pl.pallas_call(
        _even_body,
        grid_spec=grid_spec,
        out_shape=jax.ShapeDtypeStruct((n, d), F32),
        compiler_params=pltpu.CompilerParams(
            dimension_semantics=("arbitrary", "arbitrary"),
            vmem_limit_bytes=52 * MIB),
        name="even_mixer",
    )(sinks * LOG2E, x, gains, gains, w_in, conv_w, conv_b, ln_g, ln_b, bias, w_out)


def _fox_proj_body(x_ref, g_ref, wq_ref, wk_ref, wv_ref, wf_ref, bf_ref, tri_ref, eqt_ref, ek_ref,
                   qt_ref, k_ref, vt_ref, carry_ref, wqt_ref, wvt_ref):
    i = pl.program_id(1)
    ts = x_ref.shape[0]

    @pl.when(i == 0)
    def _():
        carry_ref[...] = jnp.zeros(carry_ref.shape, F32)

    @pl.when(jnp.logical_and(pl.program_id(0) == 0, i == 0))
    def _():
        for c0 in range(0, wq_ref.shape[1], 2 * LANES):
            cols = slice(c0, c0 + 2 * LANES)
            wqt_ref[cols, :] = (wq_ref[:, cols] * (LOG2E / math.sqrt(HEAD_DIM))).T.astype(BF16)
            wvt_ref[cols, :] = wv_ref[:, cols].T.astype(BF16)

    h = _rms(x_ref[...], g_ref[...]).astype(BF16)

    t = _dot(h, wf_ref[...]) + bf_ref[...]
    zqt = _dot_nt(wqt_ref[...], h)
    logf = jnp.minimum(t, 0.0) - jnp.log1p(jnp.exp(-jnp.abs(t)))
    tri = tri_ref[...]
    half = ts // 2
    total = carry_ref[0:1, :]
    halves = []
    for r0 in (0, half):
        ch = total
        for piece in _split_bf16(logf[r0:r0 + half]):
            ch = ch + _dot(tri, piece)
        total = ch[half - 1:half, :]
        halves.append(ch)
    c = jnp.concatenate(halves, axis=0)
    carry_ref[0:1, :] = total
    zk = _dot(h, wk_ref[...])

    c1, c2, c3 = _split_bf16(c * LOG2E)
    lane = lax.broadcasted_iota(jnp.int32, (ts, LANES), 1)
    cp = jnp.where(lane < FOX_H, c1,
                   jnp.where(lane < 2 * FOX_H, c2,
                             jnp.where(lane < 3 * FOX_H, c3,
                                       jnp.where(lane < 4 * FOX_H, 1.0, 0.0).astype(BF16))))
    aqt = _dot_nt(eqt_ref[...], cp)
    ak = _dot(cp, ek_ref[...])
    zvt = _dot_nt(wvt_ref[...], h)
    low = lane < HEAD_DIM
    for p in range(FOX_H // 2):
        pair = slice(p * LANES, (p + 1) * LANES)
        even = slice(2 * p * LANES, (2 * p + 1) * LANES)
        odd = slice((2 * p + 1) * LANES, (2 * p + 2) * LANES)
        k_ref[:, even] = jnp.where(low, zk[:, pair], ak[:, pair]).astype(BF16)
        k_ref[:, odd] = jnp.where(low, ak[:, pair], zk[:, pair]).astype(BF16)
    row = lax.broadcasted_iota(jnp.int32, (LANES, ts), 0)
    top = row < HEAD_DIM
    one_mid = (row == HEAD_DIM).astype(F32)
    one_top = (row == 0).astype(F32)
    for p in range(FOX_H // 2):
        pair = slice(p * LANES, (p + 1) * LANES)
        even = slice(2 * p * LANES, (2 * p + 1) * LANES)
        odd = slice((2 * p + 1) * LANES, (2 * p + 2) * LANES)
        vt_ref[even, :] = jnp.where(top, zvt[pair, :], one_mid).astype(BF16)
        vt_ref[odd, :] = jnp.where(top, one_top, zvt[pair, :]).astype(BF16)
        qt_ref[even, :] = jnp.where(top, zqt[pair, :], aqt[pair, :]).astype(BF16)
        qt_ref[odd, :] = jnp.where(top, aqt[pair, :], zqt[pair, :]).astype(BF16)


def _fox_placement():
    eq = np.zeros((LANES, FOX_H // 2 * LANES), np.float32)
    ek = np.zeros((LANES, FOX_H // 2 * LANES), np.float32)
    for h in range(FOX_H):
        base = (h // 2) * LANES + (HEAD_DIM if h % 2 == 0 else 0)
        for a in range(N_AUG):
            eq[a * FOX_H + h, base + a] = 1.0
            eq[N_AUG * FOX_H + h, base + N_AUG + a] = 1.0
            ek[N_AUG * FOX_H + h, base + a] = 1.0
            ek[a * FOX_H + h, base + N_AUG + a] = -1.0
    return eq, ek


def _fox_proj(x, batch, gains, g, layer, w_in_stack, b_f):
    n, d = x.shape
    w_in = w_in_stack[layer]
    seq = n // batch
    ts = FOX_ROWS
    nt = seq // ts
    hd = FOX_H * HEAD_DIM
    wk = w_in[:, hd:2 * hd].astype(BF16)

    def columns(block):
        return pl.BlockSpec((None, d, hd), lambda *_: (layer, 0, block), pipeline_mode=pl.Buffered(1))
    reps = N_AUG
    wf = jnp.pad(jnp.tile(w_in[:, 3 * hd:], (1, reps)), ((0, 0), (0, LANES - reps * FOX_H))).astype(BF16)
    bf = jnp.pad(jnp.tile(b_f, reps), (0, LANES - reps * FOX_H)).reshape(1, LANES).astype(F32)
    tri = jnp.asarray(np.tril(np.ones((ts // 2, ts // 2), np.float32)), BF16)
    eq, ek = _fox_placement()
    eqt, ek = jnp.asarray(eq.T, BF16), jnp.asarray(ek, BF16)
    row = pl.BlockSpec((ts, d), lambda b, i: (b * nt + i, 0))
    wide = pl.BlockSpec((ts, FOX_H * LANES), lambda b, i: (b * nt + i, 0))
    tall = pl.BlockSpec((None, FOX_H * LANES, ts), lambda b, i: (b, 0, i))
    return pl.pallas_call(
        _fox_proj_body,
        grid=(batch, nt),
        in_specs=[row, _pick_spec(g, (1, d)), columns(0), _const_spec((d, hd)),
                  columns(2), _const_spec((d, LANES)), _const_spec((1, LANES)),
                  _const_spec(tri.shape), _const_spec(eqt.shape), _const_spec(ek.shape)],
        out_specs=[tall, wide, tall],
        out_shape=[jax.ShapeDtypeStruct((batch, FOX_H * LANES, seq), BF16),
                   jax.ShapeDtypeStruct((n, FOX_H * LANES), BF16),
                   jax.ShapeDtypeStruct((batch, FOX_H * LANES, seq), BF16)],
        scratch_shapes=[pltpu.VMEM((8, LANES), F32), pltpu.VMEM((hd, d), BF16), pltpu.VMEM((hd, d), BF16)],
        compiler_params=pltpu.CompilerParams(
            dimension_semantics=("arbitrary", "arbitrary"),
            vmem_limit_bytes=52 * MIB),
        name="fox_proj",
    )(x, gains, w_in_stack, wk, w_in_stack, wf, bf, tri, eqt, ek)


def _fox_attn_body(qt_ref, k_ref, vt_ref, o_ref):
    seq = k_ref.shape[0]
    tq = FOX_TQ
    tk = FOX_TK
    per_q = tq // tk
    row = lax.broadcasted_iota(jnp.int32, (LANES, tq), 0)
    chains = [(qi, hh) for qi in reversed(range(seq // tq)) for hh in range(2)]

    def queries_of(chain, kb):
        skip = tq - tk if kb == (chain[0] + 1) * per_q - 1 else 0
        return skip, tq - skip

    def scores(chain, kb):
        qi, hh = chain
        head = slice(hh * LANES, (hh + 1) * LANES)
        skip, nq = queries_of(chain, kb)
        q0 = qi * tq + skip
        st = _dot(k_ref[kb * tk:(kb + 1) * tk, head], qt_ref[head, q0:q0 + nq])
        if kb >= qi * per_q:
            key = lax.broadcasted_iota(jnp.int32, (tk, nq), 0)
            qry = lax.broadcasted_iota(jnp.int32, (tk, nq), 1)
            st = jnp.where(key + (kb * tk - q0) <= qry, st, NEG)
        return st

    def column_max(chain, sts):
        m = None
        for kb, st in enumerate(sts):
            skip, _ = queries_of(chain, kb)
            mb = jnp.max(st, axis=0, keepdims=True)
            if skip:
                mb = jnp.concatenate([m[:, :skip], jnp.maximum(m[:, skip:], mb)], axis=1)
                m = mb
            else:
                m = mb if m is None else jnp.maximum(m, mb)
        return m

    def probabilities(chain, kb, st, m):
        skip, _ = queries_of(chain, kb)
        return jnp.exp2(st - m[:, skip:]).astype(BF16)

    def weighted_values(chain, kb, pt, acc):
        _, hh = chain
        head = slice(hh * LANES, (hh + 1) * LANES)
        skip, _ = queries_of(chain, kb)
        part = _dot(vt_ref[head, kb * tk:(kb + 1) * tk], pt)
        if acc is None:
            return part
        if skip:
            return jnp.concatenate([acc[:, :skip], acc[:, skip:] + part], axis=1)
        return acc + part

    def blocks(stage):
        return (stage[0][0] + 1) * per_q if stage is not None else 0

    done = {}
    scored = None
    exped = None
    for cur in chains + [None, None]:
        n_cur = (cur[0] + 1) * per_q if cur is not None else 0
        sts, pts, acc = [], [], None
        for kb in range(max(n_cur, blocks(scored), blocks(exped))):
            if kb < n_cur:
                sts.append(scores(cur, kb))
            if kb < blocks(scored):
                pts.append(probabilities(scored[0], kb, scored[1][kb], scored[2]))
            if kb < blocks(exped):
                acc = weighted_values(exped[0], kb, exped[1][kb], acc)
        if exped is not None:
            qi, hh = exped[0]
            ones_row = HEAD_DIM if hh == 0 else 0
            done[hh] = acc / acc[ones_row:ones_row + 1, :]
            if hh == 1:
                out = jnp.where(row < HEAD_DIM, done[0], done[1])
                o_ref[qi * tq:(qi + 1) * tq, :] = out.T.astype(BF16)
        exped = (scored[0], pts) if scored is not None else None
        scored = (cur, sts, column_max(cur, sts)) if cur is not None else None


def _fox_attn(qt, k, vt, batch):
    n = k.shape[0]
    seq = n // batch
    pairs = FOX_H // 2
    k = k.reshape(batch, seq, FOX_H * LANES)
    pair_block = pl.BlockSpec((None, seq, 2 * LANES), lambda b, p: (b, 0, p))
    out = pl.pallas_call(
        _fox_attn_body,
        grid=(batch, pairs),
        in_specs=[pl.BlockSpec((None, 2 * LANES, seq), lambda b, p: (b, p, 0)), pair_block,
                  pl.BlockSpec((None, 2 * LANES, seq), lambda b, p: (b, p, 0))],
        out_specs=pl.BlockSpec((None, seq, LANES), lambda b, p: (b, 0, p)),
        out_shape=jax.ShapeDtypeStruct((batch, seq, FOX_H * HEAD_DIM), BF16),
        compiler_params=pltpu.CompilerParams(
            dimension_semantics=("parallel", "parallel"),
            vmem_limit_bytes=48 * MIB),
        name="fox_attn",
    )(qt, k, vt)
    return out.reshape(n, FOX_H * HEAD_DIM)


def kernel(x, norm_g, ffn_w_gate, ffn_w_up, ffn_w_down, ab_w_in, conv_w, conv_b, conv_ln_g,
           conv_ln_b, swa_sinks, ab_w_out, fox_w_in, fox_b_f, fox_w_out):
    batch, seq, d = x.shape
    depth = norm_g.shape[0]
    xf = x.reshape(batch * seq, d)
    stacks = (ffn_w_gate, ffn_w_up, ffn_w_down)
    w_next = tuple(w[0, 0].astype(BF16) for w in stacks)
    n_norms = norm_g.shape[1]
    gains = norm_g.reshape(depth * n_norms, 1, d)
    conv_b, conv_ln_g, conv_ln_b = (p.reshape(p.shape[0], 1, -1) for p in (conv_b, conv_ln_g, conv_ln_b))
    for l in range(depth):
        g = l * n_norms
        xf, w_next = _ffn(xf, gains, g, g + 1, w_next, cast=(stacks, (l, 1)))
        i = l // 2
        mix = None
        if l % 2 == 0:
            xf = _even_mixer(xf, batch, gains, g + 2, g + 3, i, ab_w_in, conv_w, conv_b,
                             conv_ln_g, conv_ln_b, swa_sinks[i], ab_w_out)
        else:
            qt, k, vt = _fox_proj(xf, batch, gains, g + 2, i, fox_w_in, fox_b_f[i])
            mix = (_fox_attn(qt, k, vt, batch), fox_w_out[i], g + 3)
        cast = (stacks, (l + 1, 0)) if l + 1 < depth else None
        xf, w_next = _ffn(xf, gains, g + 4, g + 5, w_next, mix, cast)
    return xf.reshape(batch, seq, d)
```
